```python
import math
import jax, jax.numpy as jnp
from jax import lax
import numpy as np

D_MODEL = 2048
BATCH = 1
SEQ = 8192
DEPTH = 2

EPS = 1e-6
BLK = 128
DA_HEADS = 8
DA_QK_DIM = 64
DA_V_DIM = 2 * DA_QK_DIM
DA_WIDTH = DA_HEADS * DA_V_DIM
RET_HEADS = 8
RET_QK_DIM = 64
RET_V_DIM = 128
RET_WIDTH = RET_HEADS * RET_V_DIM
RET_CHUNK = 128
ROT_BASE = 10000.0
DIL_CONFIGS = ((128, 1), (512, 4), (2048, 16))
DIL_GROUPS = 3
DIL_HEADS = 8
DIL_HEAD_DIM = 128
DIL_WIDTH = DIL_HEADS * DIL_HEAD_DIM
DIL_QKV = DIL_GROUPS * DIL_HEADS * DIL_HEAD_DIM
IN_SPLITS = (
    DA_HEADS * 2 * DA_QK_DIM, DA_HEADS * 2 * DA_QK_DIM, DA_WIDTH, DA_WIDTH,
    RET_HEADS * RET_QK_DIM, RET_HEADS * RET_QK_DIM, RET_WIDTH, RET_WIDTH,
    DIL_QKV, DIL_QKV, DIL_QKV, DIL_WIDTH,
    D_MODEL, D_MODEL, D_MODEL,
)
IN_WIDTH = sum(IN_SPLITS)

kernel_name = "hybrid_diffattn_retention_dilated_gated"


def rms_norm(x, w):
    xf = x.astype(jnp.float32)
    y = xf * lax.rsqrt(jnp.mean(xf * xf, axis=-1, keepdims=True) + EPS)
    return (y * w.astype(jnp.float32)).astype(x.dtype)


def diff_attention(q, k, v, lam, lam_init, norm_w):
    B, S, H, _, dq = q.shape
    dv = v.shape[-1]
    lf = lam.astype(jnp.float32)
    lam_full = jnp.exp(jnp.sum(lf[0] * lf[1])) - jnp.exp(jnp.sum(lf[2] * lf[3])) + lam_init
    qf = q.astype(jnp.float32) * (dq ** -0.5)
    kf = k.astype(jnp.float32)
    vf = v.astype(jnp.float32)
    nb = S // BLK
    q_blocks = jnp.moveaxis(qf.reshape(B, nb, BLK, H, 2, dq), 1, 0)
    k_pos = jnp.arange(S)

    def one_block(args):
        qb, start = args
        s = jnp.einsum('bqhmd,bkhmd->bhmqk', qb, kf)
        q_pos = start + jnp.arange(BLK)
        causal = k_pos[None, :] <= q_pos[:, None]
        p = jax.nn.softmax(jnp.where(causal, s, -jnp.inf), axis=-1)
        a = p[:, :, 0] - lam_full * p[:, :, 1]
        return jnp.einsum('bhqk,bkhd->bqhd', a, vf)

    o = lax.map(one_block, (q_blocks, jnp.arange(nb) * BLK))
    o = jnp.moveaxis(o, 0, 1).reshape(B, S, H, dv)
    o = rms_norm(o, norm_w) * (1.0 - lam_init)
    return o.reshape(B, S, H * dv)


def rotate_pairs(x, pos):
    d = x.shape[-1]
    theta = 1.0 / (ROT_BASE ** jnp.linspace(0.0, 1.0, d // 2, dtype=jnp.float32))
    ang = pos.astype(jnp.float32)[:, None] * theta[None, :]
    cos = jnp.repeat(jnp.cos(ang), 2, axis=-1)[None, :, None, :]
    sin = jnp.repeat(jnp.sin(ang), 2, axis=-1)[None, :, None, :]
    rot = jnp.stack([-x[..., 1::2], x[..., 0::2]], axis=-1).reshape(x.shape)
    return x * cos + rot * sin


def retention(q, k, v, norm_w):
    B, S, H, dk = q.shape
    dv = v.shape[-1]
    C = RET_CHUNK
    N = S // C
    pos = jnp.arange(S)
    q = rotate_pairs(q.astype(jnp.float32), pos)
    k = rotate_pairs(k.astype(jnp.float32), pos) * (dk ** -0.5)
    v = v.astype(jnp.float32)
    log_g = jnp.log1p(-jnp.exp2(-5.0 - jnp.arange(H, dtype=jnp.float32)))
    qc = q.reshape(B, N, C, H, dk)
    kc = k.reshape(B, N, C, H, dk)
    vc = v.reshape(B, N, C, H, dv)
    i = jnp.arange(C, dtype=jnp.float32)
    diff = i[:, None] - i[None, :]
    decay = jnp.where(diff >= 0, jnp.exp(jnp.maximum(diff, 0.0)[None] * log_g[:, None, None]), 0.0)
    scores = jnp.einsum('bnihd,bnjhd->bnhij', qc, kc) * decay[None, None]
    o_inner = jnp.einsum('bnhij,bnjhe->bnihe', scores, vc)
    k_dec = jnp.exp((C - 1.0 - i)[:, None] * log_g[None, :])
    kv = jnp.einsum('bnjhd,bnjhe->nbhde', kc * k_dec[None, None, :, :, None], vc)
    chunk_decay = jnp.exp(C * log_g)[None, :, None, None]

    def step(R, kv_n):
        return chunk_decay * R + kv_n, R

    _, R_prev = lax.scan(step, jnp.zeros((B, H, dk, dv), jnp.float32), kv)
    q_dec = jnp.exp((i + 1.0)[:, None] * log_g[None, :])
    o_cross = jnp.einsum('bnihd,nbhde->bnihe', qc * q_dec[None, None, :, :, None], R_prev)
    o = (o_inner + o_cross).reshape(B, S, H, dv)
    o = rms_norm(o, norm_w)
    return o.reshape(B, S, H * dv)


def dilated_group(q, k, v, window, dilation):
    B, S, H, dh = q.shape
    radius = window // dilation
    span = dilation * BLK
    L = -(-S // span) * span
    M = L // dilation
    nb = M // BLK

    def to_sub(t):
        t = jnp.pad(t.astype(jnp.float32), ((0, 0), (0, L - S), (0, 0), (0, 0)))
        t = t.reshape(B, M, dilation, H, dh).transpose(0, 2, 3, 1, 4)
        return t.reshape(B, dilation, H, nb, BLK, dh)

    qs, ks, vs = to_sub(q), to_sub(k), to_sub(v)

    def with_prev(t):
        prev = jnp.pad(t[:, :, :, :-1], ((0, 0), (0, 0), (0, 0), (1, 0), (0, 0), (0, 0)))
        return jnp.concatenate([prev, t], axis=4)

    kk, vv = with_prev(ks), with_prev(vs)
    s = jnp.einsum('brhnqd,brhnkd->brhnqk', qs * (dh ** -0.5), kk)
    blk_idx = jnp.arange(nb)[:, None, None]
    qi = jnp.arange(BLK)[None, :, None]
    kj = jnp.arange(2 * BLK)[None, None, :]
    dist = BLK + qi - kj
    valid = (dist >= 0) & (dist <= radius) & ((blk_idx > 0) | (kj >= BLK))
    s = jnp.where(valid, s, -jnp.inf)
    lse = jax.nn.logsumexp(s, axis=-1)
    p = jnp.exp(s - lse[..., None])
    o = jnp.einsum('brhnqk,brhnkd->brhnqd', p, vv)
    o = o.reshape(B, dilation, H, M, dh).transpose(0, 3, 1, 2, 4).reshape(B, L, H, dh)[:, :S]
    lse = lse.reshape(B, dilation, H, M).transpose(0, 3, 1, 2).reshape(B, L, H)[:, :S]
    return o, lse


def dilated_mixture(q, k, v):
    B, S, G, H, dh = q.shape
    outs, lses = [], []
    for g, (w, d) in enumerate(DIL_CONFIGS):
        o, l = dilated_group(q[:, :, g], k[:, :, g], v[:, :, g], w, d)
        outs.append(o)
        lses.append(l)
    wts = jax.nn.softmax(jnp.stack(lses, axis=0), axis=0)
    o = jnp.sum(wts[..., None] * jnp.stack(outs, axis=0), axis=0)
    return o.reshape(B, S, H * dh)


def hybrid_layer(x, c, layer, w_in, w_proj_a, w_proj_b, w_proj_c, w_out, w_ada, b_ada,
                 norm_w, lam, da_norm_w, ret_norm_w):
    B, S, _ = x.shape
    dt = x.dtype
    mod = jnp.einsum('bd,de->be', jax.nn.silu(c), w_ada) + b_ada
    shift, scale, gate = jnp.split(mod, 3, axis=-1)
    u = rms_norm(x, norm_w) * (1.0 + scale[:, None]) + shift[:, None]
    proj = jnp.einsum('bsd,de->bse', u, w_in)
    offsets = np.cumsum(IN_SPLITS)[:-1].tolist()
    (a_q, a_k, a_v, a_z, b_q, b_k, b_v, b_z,
     c_q, c_k, c_v, c_z, g_a, g_b, g_c) = jnp.split(proj, offsets, axis=-1)

    lam_init = 0.8 - 0.6 * math.exp(-0.3 * layer)
    o_a = diff_attention(a_q.reshape(B, S, DA_HEADS, 2, DA_QK_DIM),
                         a_k.reshape(B, S, DA_HEADS, 2, DA_QK_DIM),
                         a_v.reshape(B, S, DA_HEADS, DA_V_DIM), lam, lam_init, da_norm_w)
    o_b = retention(b_q.reshape(B, S, RET_HEADS, RET_QK_DIM),
                    b_k.reshape(B, S, RET_HEADS, RET_QK_DIM),
                    b_v.reshape(B, S, RET_HEADS, RET_V_DIM), ret_norm_w)
    dil_shape = (B, S, DIL_GROUPS, DIL_HEADS, DIL_HEAD_DIM)
    o_c = dilated_mixture(c_q.reshape(dil_shape), c_k.reshape(dil_shape), c_v.reshape(dil_shape))

    y_a = jnp.einsum('bsw,wd->bsd', o_a.astype(dt) * jax.nn.silu(a_z), w_proj_a)
    y_b = jnp.einsum('bsw,wd->bsd', o_b.astype(dt) * jax.nn.silu(b_z), w_proj_b)
    y_c = jnp.einsum('bsw,wd->bsd', o_c.astype(dt) * jax.nn.silu(c_z), w_proj_c)
    merged = jax.nn.sigmoid(g_a) * y_a + jax.nn.sigmoid(g_b) * y_b + jax.nn.sigmoid(g_c) * y_c
    y = jnp.einsum('bsd,de->bse', merged, w_out)
    return x + gate[:, None] * y


def setup_inputs(seed: int = 0) -> dict:
    key = jax.random.key(seed)
    ks = jax.random.split(key, 16)
    f32 = jnp.float32
    D = D_MODEL
    x = jax.random.normal(ks[0], (BATCH, SEQ, D), f32)
    c = jax.random.normal(ks[1], (BATCH, D), f32)
    w_in = jax.random.normal(ks[2], (DEPTH, D, IN_WIDTH), f32) * D ** -0.5
    w_proj_a = jax.random.normal(ks[3], (DEPTH, DA_WIDTH, D), f32) * DA_WIDTH ** -0.5
    w_proj_b = jax.random.normal(ks[4], (DEPTH, RET_WIDTH, D), f32) * RET_WIDTH ** -0.5
    w_proj_c = jax.random.normal(ks[5], (DEPTH, DIL_WIDTH, D), f32) * DIL_WIDTH ** -0.5
    w_out = jax.random.normal(ks[6], (DEPTH, D, D), f32) * D ** -0.5
    w_ada = jax.random.normal(ks[7], (DEPTH, D, 3 * D), f32) * (0.5 * D ** -0.5)
    b_ada = jax.random.normal(ks[8], (DEPTH, 3 * D), f32) * 0.01
    norm_w = 1.0 + 0.02 * jax.random.normal(ks[9], (DEPTH, D), f32)
    lam = 0.1 * jax.random.normal(ks[10], (DEPTH, 4, DA_QK_DIM), f32)
    da_norm_w = 1.0 + 0.02 * jax.random.normal(ks[11], (DEPTH, DA_V_DIM), f32)
    ret_norm_w = 1.0 + 0.02 * jax.random.normal(ks[12], (DEPTH, RET_V_DIM), f32)
    final_norm_w = 1.0 + 0.02 * jax.random.normal(ks[13], (D,), f32)
    return {"x": x, "c": c, "w_in": w_in, "w_proj_a": w_proj_a, "w_proj_b": w_proj_b,
            "w_proj_c": w_proj_c, "w_out": w_out, "w_ada": w_ada, "b_ada": b_ada,
            "norm_w": norm_w, "lam": lam, "da_norm_w": da_norm_w, "ret_norm_w": ret_norm_w,
            "final_norm_w": final_norm_w}


def reference(x, c, w_in, w_proj_a, w_proj_b, w_proj_c, w_out, w_ada, b_ada, norm_w, lam,
              da_norm_w, ret_norm_w, final_norm_w):
    h = x
    for layer in range(DEPTH):
        h = hybrid_layer(h, c, layer, w_in[layer], w_proj_a[layer], w_proj_b[layer],
                         w_proj_c[layer], w_out[layer], w_ada[layer], b_ada[layer],
                         norm_w[layer], lam[layer], da_norm_w[layer], ret_norm_w[layer])
    return rms_norm(h, final_norm_w)
```

```python
import functools
import math

import numpy as np
import jax
import jax.numpy as jnp
from jax import lax
from jax.experimental import pallas as pl
from jax.experimental.pallas import tpu as pltpu

F32 = jnp.float32
BF16 = jnp.bfloat16

LANES = 128
VMEM_LIMIT_BYTES = 56 * 1024 * 1024

EPS = 1e-6
LOG2E = math.log2(math.e)

D_MODEL = 2048
DA_HEADS = 8
DA_QK_DIM = 64
DA_V_DIM = 128
RET_HEADS = 8
RET_QK_DIM = 64
RET_V_DIM = 128
RET_CHUNK = 256
ROT_BASE = 10000.0
DIL_DILATIONS = (1, 4, 16)
DIL_RADIUS = 128
DIL_HEADS = 8
DIL_HEAD_DIM = 128
DIL_TILE = 2048
MASK_VALUE = -1e30

CB_AQ, CB_AK, CB_AV, CB_AZ = 0, 8, 16, 24
CB_BQ, CB_BK, CB_BV, CB_BZ = 32, 36, 40, 48
CB_CQ, CB_CK, CB_CV, CB_CZ = 56, 80, 104, 128
CB_GA, CB_GB, CB_GC = 136, 152, 168
GATE_CB = 8
IN_WIDTH = 184 * LANES


def _params(*sem):
    return pltpu.CompilerParams(dimension_semantics=sem, vmem_limit_bytes=VMEM_LIMIT_BYTES)


def _silu(z):
    return z * jax.nn.sigmoid(z)


def _adaln_body(c_ref, w_ref, b_ref, o_ref):
    cc = c_ref[...]
    o_ref[0] = jnp.sum(_silu(cc) * w_ref[0], axis=0, keepdims=True) + b_ref[0]


def _adaln(c, w_ada, b_ada):
    n_layers, d, e = w_ada.shape
    tn = 768
    return pl.pallas_call(
        _adaln_body,
        grid=(n_layers, e // tn),
        in_specs=[
            pl.BlockSpec((d, 1), lambda l, j: (0, 0)),
            pl.BlockSpec((1, d, tn), lambda l, j: (l, 0, j)),
            pl.BlockSpec((1, 1, tn), lambda l, j: (l, 0, j)),
        ],
        out_specs=pl.BlockSpec((1, 1, tn), lambda l, j: (l, 0, j)),
        out_shape=jax.ShapeDtypeStruct((n_layers, 1, e), F32),
        compiler_params=_params("arbitrary", "arbitrary"),
        name="adaln",
    )(c.reshape(d, 1), w_ada, b_ada.reshape(n_layers, 1, e))


def _inproj_body(x_ref, nw_ref, shift_ref, scale_ref, w_ref, o_ref, u_ref, *, tn, a_scale, c_scale):
    j = pl.program_id(1)

    @pl.when(j == 0)
    def _():
        xf = x_ref[...]
        y = xf * lax.rsqrt(jnp.mean(xf * xf, axis=-1, keepdims=True) + EPS) * nw_ref[...]
        u_ref[...] = (y * (1.0 + scale_ref[0]) + shift_ref[0]).astype(BF16)

    acc = jnp.dot(u_ref[...], w_ref[...].astype(BF16), preferred_element_type=F32)
    cb = j * (tn // LANES)
    qscale = jnp.where(cb < CB_AK, a_scale, jnp.where((cb >= CB_CQ) & (cb < CB_CK), c_scale, 1.0))
    acc = acc * qscale
    for cblk in range(tn // LANES):
        o_ref[cblk] = acc[:, cblk * LANES:(cblk + 1) * LANES].astype(BF16)


def _inproj(x, norm_w, mod, layer, w_in):
    s, d = x.shape
    tm, tn = 1024, 512
    body = functools.partial(
        _inproj_body, tn=tn,
        a_scale=DA_QK_DIM ** -0.5 * LOG2E, c_scale=DIL_HEAD_DIM ** -0.5 * LOG2E)
    return pl.pallas_call(
        body,
        grid=(s // tm, IN_WIDTH // tn),
        in_specs=[
            pl.BlockSpec((tm, d), lambda i, j: (i, 0)),
            pl.BlockSpec((1, d), lambda i, j: (0, 0)),
            pl.BlockSpec((1, 1, d), lambda i, j: (layer, 0, 0)),
            pl.BlockSpec((1, 1, d), lambda i, j: (layer, 0, 1)),
            pl.BlockSpec((None, d, tn), lambda i, j: (layer, 0, j)),
        ],
        out_specs=pl.BlockSpec((tn // LANES, tm, LANES), lambda i, j: (j, i, 0)),
        out_shape=jax.ShapeDtypeStruct((IN_WIDTH // LANES, s, LANES), BF16),
        scratch_shapes=[pltpu.VMEM((tm, d), BF16)],
        compiler_params=_params("arbitrary", "arbitrary"),
        name="inproj",
    )(x, norm_w.reshape(1, d), mod, mod, w_in)


def _diffattn_body(lam_ref, nw_ref, q_ref, k_ref, v_ref, z_ref, o_ref, qs_ref, m_ref, l_ref, acc_ref,
                   *, tq, lam_init):
    i = pl.program_id(1)
    q = q_ref[0]
    lane = lax.broadcasted_iota(jnp.int32, (tq, LANES), 1)
    zero = jnp.zeros_like(q)
    qs_ref[0:tq] = jnp.where(lane < DA_QK_DIM, q, zero)
    qs_ref[tq:2 * tq] = jnp.where(lane >= DA_QK_DIM, q, zero)
    m_ref[...] = jnp.full(m_ref.shape, -jnp.inf, F32)
    l_ref[...] = jnp.zeros(l_ref.shape, F32)
    acc_ref[...] = jnp.zeros(acc_ref.shape, F32)

    def step(kt, diagonal):
        start = pl.multiple_of(kt * tq, tq)
        k = k_ref[0, pl.ds(start, tq), :]
        v = v_ref[0, pl.ds(start, tq), :]
        s = lax.dot_general(qs_ref[...], k, (((1,), (1,)), ((), ())), preferred_element_type=F32)
        if diagonal:
            row = lax.broadcasted_iota(jnp.int32, s.shape, 0)
            col = lax.broadcasted_iota(jnp.int32, s.shape, 1)
            s = jnp.where(col <= jnp.where(row >= tq, row - tq, row), s, -jnp.inf)
        m_prev = m_ref[...]
        m_new = jnp.maximum(m_prev, jnp.max(s, axis=-1, keepdims=True))
        alpha = jnp.exp2(m_prev - m_new)
        p = jnp.exp2(s - m_new)
        l_ref[...] = alpha * l_ref[...] + jnp.sum(p, axis=-1, keepdims=True)
        acc_ref[...] = alpha * acc_ref[...] + jnp.dot(p.astype(BF16), v, preferred_element_type=F32)
        m_ref[...] = m_new

    def off_diagonal(kt, carry):
        step(kt, False)
        return carry

    lax.fori_loop(0, i, off_diagonal, 0)
    step(i, True)

    lam = lam_ref[...]
    lam_full = (jnp.exp(jnp.sum(lam[0:1] * lam[1:2], axis=-1, keepdims=True))
                - jnp.exp(jnp.sum(lam[2:3] * lam[3:4], axis=-1, keepdims=True)) + lam_init)
    o = acc_ref[0:tq] / l_ref[0:tq] - lam_full * (acc_ref[tq:2 * tq] / l_ref[tq:2 * tq])
    o = o * lax.rsqrt(jnp.mean(o * o, axis=-1, keepdims=True) + EPS) * nw_ref[...] * (1.0 - lam_init)
    o_ref[...] = (o * _silu(z_ref[0].astype(F32))).astype(BF16)


def _diffattn(proj, lam, norm_w, lam_init):
    s = proj.shape[1]
    tq = 512
    body = functools.partial(_diffattn_body, tq=tq, lam_init=lam_init)
    return pl.pallas_call(
        body,
        grid=(DA_HEADS, s // tq),
        in_specs=[
            pl.BlockSpec((4, DA_QK_DIM), lambda h, i: (0, 0)),
            pl.BlockSpec((1, DA_V_DIM), lambda h, i: (0, 0)),
            pl.BlockSpec((1, tq, LANES), lambda h, i: (CB_AQ + h, i, 0)),
            pl.BlockSpec((1, s, LANES), lambda h, i: (CB_AK + h, 0, 0)),
            pl.BlockSpec((1, s, LANES), lambda h, i: (CB_AV + h, 0, 0)),
            pl.BlockSpec((1, tq, LANES), lambda h, i: (CB_AZ + h, i, 0)),
        ],
        out_specs=pl.BlockSpec((tq, DA_V_DIM), lambda h, i: (i, h)),
        out_shape=jax.ShapeDtypeStruct((s, DA_HEADS * DA_V_DIM), BF16),
        scratch_shapes=[
            pltpu.VMEM((2 * tq, LANES), BF16),
            pltpu.VMEM((2 * tq, 1), F32),
            pltpu.VMEM((2 * tq, 1), F32),
            pltpu.VMEM((2 * tq, DA_V_DIM), F32),
        ],
        compiler_params=_params("arbitrary", "arbitrary"),
        name="diffattn",
    )(lam, norm_w.reshape(1, DA_V_DIM), proj, proj, proj, proj)


def _retention_tables(s):
    c = RET_CHUNK
    half = RET_QK_DIM // 2
    theta = 1.0 / (ROT_BASE ** jnp.linspace(0.0, 1.0, half, dtype=F32))
    ang = jnp.arange(s, dtype=F32)[:, None] * theta[None, :]
    cos = jnp.tile(jnp.repeat(jnp.cos(ang), 2, axis=-1), (1, 2))
    sign = jnp.tile(jnp.asarray([-1.0, 1.0], F32), RET_QK_DIM)
    sin = jnp.tile(jnp.repeat(jnp.sin(ang), 2, axis=-1), (1, 2)) * sign[None, :]
    log_g = jnp.log1p(-jnp.exp2(-5.0 - jnp.arange(RET_HEADS, dtype=F32)))
    idx = jnp.arange(c, dtype=F32)
    lane_g = jnp.repeat(log_g.reshape(RET_HEADS // 2, 2), RET_QK_DIM, axis=-1)
    qdec = jnp.exp((idx + 1.0)[None, :, None] * lane_g[:, None, :])
    kdec = jnp.exp((c - 1.0 - idx)[None, :, None] * lane_g[:, None, :]) * RET_QK_DIM ** -0.5
    cdec = jnp.broadcast_to(jnp.exp(c * lane_g)[:, :, None], (RET_HEADS // 2, LANES, RET_V_DIM))
    causal = idx[:, None] >= idx[None, :]
    dmask = jnp.where(causal[None], jnp.exp(-c * log_g)[:, None, None], 0.0)
    return cos, sin, qdec, kdec, cdec, dmask.astype(F32)


def _retention_body(cos_ref, sin_ref, qdec_ref, kdec_ref, cdec_ref, dmask_ref, nw_ref,
                    q_ref, k_ref, v_ref, z_ref, o_ref, r_ref):
    @pl.when(pl.program_id(1) == 0)
    def _():
        r_ref[...] = jnp.zeros(r_ref.shape, F32)

    c = q_ref.shape[1]
    cos = cos_ref[...]
    sin = sin_ref[...]
    lane = lax.broadcasted_iota(jnp.int32, (c, LANES), 1)
    even = (lane & 1) == 0
    head0 = lane < RET_QK_DIM

    def rotate(x):
        nxt = pltpu.roll(x, LANES - 1, 1)
        prv = pltpu.roll(x, 1, 1)
        return x * cos + jnp.where(even, nxt, prv) * sin

    qd = (rotate(q_ref[0].astype(F32)) * qdec_ref[0]).astype(BF16)
    kd = (rotate(k_ref[0].astype(F32)) * kdec_ref[0]).astype(BF16)
    zero = jnp.zeros_like(qd)
    state = r_ref[...]
    state_b = state.astype(BF16)
    updates = []
    for hh in range(2):
        qz = jnp.where(head0, qd, zero) if hh == 0 else jnp.where(head0, zero, qd)
        sc = lax.dot_general(qz, kd, (((1,), (1,)), ((), ())), preferred_element_type=F32) * dmask_ref[hh]
        v = v_ref[hh]
        o = (jnp.dot(sc.astype(BF16), v, preferred_element_type=F32)
             + jnp.dot(qz, state_b, preferred_element_type=F32))
        o = o * lax.rsqrt(jnp.mean(o * o, axis=-1, keepdims=True) + EPS) * nw_ref[...]
        o_ref[:, hh * RET_V_DIM:(hh + 1) * RET_V_DIM] = (o * _silu(z_ref[hh].astype(F32))).astype(BF16)
        updates.append(lax.dot_general(kd, v, (((0,), (0,)), ((), ())), preferred_element_type=F32))
    row = lax.broadcasted_iota(jnp.int32, state.shape, 0)
    r_ref[...] = cdec_ref[0] * state + jnp.where(row < RET_QK_DIM, updates[0], updates[1])


def _retention(proj, norm_w):
    s = proj.shape[1]
    c = RET_CHUNK
    pairs = RET_HEADS // 2
    cos, sin, qdec, kdec, cdec, dmask = _retention_tables(s)
    return pl.pallas_call(
        _retention_body,
        grid=(pairs, s // c),
        in_specs=[
            pl.BlockSpec((c, LANES), lambda p, n: (n, 0)),
            pl.BlockSpec((c, LANES), lambda p, n: (n, 0)),
            pl.BlockSpec((1, c, LANES), lambda p, n: (p, 0, 0)),
            pl.BlockSpec((1, c, LANES), lambda p, n: (p, 0, 0)),
            pl.BlockSpec((1, LANES, RET_V_DIM), lambda p, n: (p, 0, 0)),
            pl.BlockSpec((2, c, c), lambda p, n: (p, 0, 0)),
            pl.BlockSpec((1, RET_V_DIM), lambda p, n: (0, 0)),
            pl.BlockSpec((1, c, LANES), lambda p, n: (CB_BQ + p, n, 0)),
            pl.BlockSpec((1, c, LANES), lambda p, n: (CB_BK + p, n, 0)),
            pl.BlockSpec((2, c, LANES), lambda p, n: (CB_BV // 2 + p, n, 0)),
            pl.BlockSpec((2, c, LANES), lambda p, n: (CB_BZ // 2 + p, n, 0)),
        ],
        out_specs=pl.BlockSpec((c, 2 * RET_V_DIM), lambda p, n: (n, p)),
        out_shape=jax.ShapeDtypeStruct((s, RET_HEADS * RET_V_DIM), BF16),
        scratch_shapes=[pltpu.VMEM((LANES, RET_V_DIM), F32)],
        compiler_params=_params("arbitrary", "arbitrary"),
        name="retention",
    )(cos, sin, qdec, kdec, cdec, dmask, norm_w.reshape(1, RET_V_DIM), proj, proj, proj, proj)


def _dilated_body(*refs):
    (q0, k0, v0, q1, k1, v1, q2, k2, v2, z_ref, o_ref,
     slab, qp, kp0, vp0, kp1, vp1, kp2, vp2, og0, og1, og2, ls0, ls1, ls2) = refs
    t = pl.program_id(1)
    blk = DIL_RADIUS
    groups = (
        (1, q0, k0, v0, kp0, vp0, og0, ls0),
        (4, q1, k1, v1, kp1, vp1, og1, ls1),
        (16, q2, k2, v2, kp2, vp2, og2, ls2),
    )

    qi = lax.broadcasted_iota(jnp.int32, (blk, 2 * blk), 0)
    kj = lax.broadcasted_iota(jnp.int32, (blk, 2 * blk), 1)
    in_window = (kj >= qi) & (kj <= qi + blk)
    bias = jnp.where(in_window, 0.0, MASK_VALUE).astype(F32)
    bias_first = jnp.where(in_window & (kj >= blk), 0.0, MASK_VALUE).astype(F32)

    for d, q_ref, k_ref, v_ref, kp, vp, og, ls in groups:
        m_t = DIL_TILE // d
        nb = m_t // blk

        @pl.when(t == 0)
        def _():
            kp[:, 0:blk, :] = jnp.zeros((d, blk, LANES), BF16)
            vp[:, 0:blk, :] = jnp.zeros((d, blk, LANES), BF16)

        if d == 1:
            qp[...] = q_ref[0]
            kp[0, blk:blk + m_t, :] = k_ref[0]
            vp[0, blk:blk + m_t, :] = v_ref[0]
        else:
            slab[...] = q_ref[0].astype(F32)
            for r in range(d):
                qp[r * m_t:(r + 1) * m_t, :] = slab[pl.ds(r, m_t, stride=d), :].astype(BF16)
            slab[...] = k_ref[0].astype(F32)
            for r in range(d):
                kp[r, blk:blk + m_t, :] = slab[pl.ds(r, m_t, stride=d), :].astype(BF16)
            slab[...] = v_ref[0].astype(F32)
            for r in range(d):
                vp[r, blk:blk + m_t, :] = slab[pl.ds(r, m_t, stride=d), :].astype(BF16)

        def unit(u, carry, d=d, m_t=m_t, nb=nb, kp=kp, vp=vp, og=og, ls=ls):
            r = u // nb
            b = u % nb
            qu = qp[pl.ds(pl.multiple_of(r * m_t + b * blk, blk), blk), :]
            start = pl.multiple_of(b * blk, blk)
            kw = kp[r, pl.ds(start, 2 * blk), :]
            vw = vp[r, pl.ds(start, 2 * blk), :]
            s = lax.dot_general(qu, kw, (((1,), (1,)), ((), ())), preferred_element_type=F32)
            s = s + jnp.where((t == 0) & (b == 0), bias_first, bias)
            m = jnp.max(s, axis=-1, keepdims=True)
            p = jnp.exp2(s - m)
            l = jnp.sum(p, axis=-1, keepdims=True)
            o = jnp.dot(p.astype(BF16), vw, preferred_element_type=F32) / l
            lse = jnp.broadcast_to(m + jnp.log2(l), (blk, LANES))
            if d == 1:
                row0 = pl.multiple_of(b * blk, blk)
                og[pl.ds(row0, blk), :] = o
                ls[pl.ds(row0, blk), :] = lse
            else:
                row0 = b * (blk * d) + r
                og[pl.ds(row0, blk, stride=d), :] = o
                ls[pl.ds(row0, blk, stride=d), :] = lse
            return carry

        lax.fori_loop(0, d * nb, unit, 0)

        kp[:, 0:blk, :] = kp[:, m_t:m_t + blk, :]
        vp[:, 0:blk, :] = vp[:, m_t:m_t + blk, :]

    l0, l1, l2 = ls0[...], ls1[...], ls2[...]
    mx = jnp.maximum(jnp.maximum(l0, l1), l2)
    w0, w1, w2 = jnp.exp2(l0 - mx), jnp.exp2(l1 - mx), jnp.exp2(l2 - mx)
    o = (w0 * og0[...] + w1 * og1[...] + w2 * og2[...]) / (w0 + w1 + w2)
    o_ref[...] = (o * _silu(z_ref[0].astype(F32))).astype(BF16)


def _dilated(proj):
    s = proj.shape[1]
    tile = DIL_TILE
    blk = DIL_RADIUS

    def qkv_spec(base, g):
        return pl.BlockSpec((1, tile, LANES), lambda h, t: (base + g * DIL_HEADS + h, t, 0))

    in_specs = []
    for g in range(len(DIL_DILATIONS)):
        in_specs += [qkv_spec(CB_CQ, g), qkv_spec(CB_CK, g), qkv_spec(CB_CV, g)]
    in_specs.append(pl.BlockSpec((1, tile, LANES), lambda h, t: (CB_CZ + h, t, 0)))
    scratch = [pltpu.VMEM((tile, LANES), F32), pltpu.VMEM((tile, LANES), BF16)]
    for d in DIL_DILATIONS:
        scratch += [pltpu.VMEM((d, blk + tile // d, LANES), BF16)] * 2
    scratch += [pltpu.VMEM((tile, LANES), F32)] * 6
    return pl.pallas_call(
        _dilated_body,
        grid=(DIL_HEADS, s // tile),
        in_specs=in_specs,
        out_specs=pl.BlockSpec((tile, DIL_HEAD_DIM), lambda h, t: (t, h)),
        out_shape=jax.ShapeDtypeStruct((s, DIL_HEADS * DIL_HEAD_DIM), BF16),
        scratch_shapes=scratch,
        compiler_params=_params("arbitrary", "arbitrary"),
        name="dilated",
    )(*([proj] * 10))


def _merge_body(*refs, final):
    (ga_ref, gb_ref, gc_ref, ta0, ta1, tb0, tb1, tc0, tc1, pa_ref, pb_ref, pc_ref, wo_ref,
     h_ref, gate_ref, fw_ref, o_ref) = refs

    def branch(g_ref, t_refs, p_ref):
        y = jnp.dot(g_ref[...], p_ref[...], preferred_element_type=F32)
        gate = jnp.concatenate([t[cblk] for t in t_refs for cblk in range(t.shape[0])], axis=-1).astype(F32)
        return jax.nn.sigmoid(gate) * y

    merged = (branch(ga_ref, (ta0, ta1), pa_ref) + branch(gb_ref, (tb0, tb1), pb_ref)
              + branch(gc_ref, (tc0, tc1), pc_ref))
    y = jnp.dot(merged.astype(BF16), wo_ref[...], preferred_element_type=F32)
    out = h_ref[...] + gate_ref[0] * y
    if final:
        out = out * lax.rsqrt(jnp.mean(out * out, axis=-1, keepdims=True) + EPS) * fw_ref[...]
    o_ref[...] = out


def _merge(ga, gb, gc, proj, pa, pb, pc, wo, h, mod, layer, final_w, final):
    s, d = h.shape
    w = ga.shape[1]
    tm = 256
    const = dict(pipeline_mode=pl.Buffered(1))
    return pl.pallas_call(
        functools.partial(_merge_body, final=final),
        grid=(s // tm,),
        in_specs=[
            pl.BlockSpec((tm, w), lambda i: (i, 0)),
            pl.BlockSpec((tm, w), lambda i: (i, 0)),
            pl.BlockSpec((tm, w), lambda i: (i, 0)),
            *[pl.BlockSpec((GATE_CB, tm, LANES), functools.partial(lambda i, b: (b, i, 0), b=CB_GA // GATE_CB + b))
              for b in range(3 * d // LANES // GATE_CB)],
            pl.BlockSpec((w, d), lambda i: (0, 0), **const),
            pl.BlockSpec((w, d), lambda i: (0, 0), **const),
            pl.BlockSpec((w, d), lambda i: (0, 0), **const),
            pl.BlockSpec((d, d), lambda i: (0, 0), **const),
            pl.BlockSpec((tm, d), lambda i: (i, 0)),
            pl.BlockSpec((1, 1, d), lambda i: (layer, 0, 2)),
            pl.BlockSpec((1, d), lambda i: (0, 0)),
        ],
        out_specs=pl.BlockSpec((tm, d), lambda i: (i, 0)),
        out_shape=jax.ShapeDtypeStruct((s, d), F32),
        compiler_params=_params("arbitrary"),
        name="merge",
    )(ga, gb, gc, *([proj] * 6), pa, pb, pc, wo, h, mod, final_w.reshape(1, d))


def kernel(x, c, w_in, w_proj_a, w_proj_b, w_proj_c, w_out, w_ada, b_ada, norm_w, lam, da_norm_w, ret_norm_w,
           final_norm_w):
    batch, s, d = x.shape
    assert batch == 1 and d == D_MODEL and s % DIL_TILE == 0 and w_in.shape[2] == IN_WIDTH
    depth = w_in.shape[0]
    mod = _adaln(c, w_ada, b_ada)
    h = x[0]
    for layer in range(depth):
        lam_init = 0.8 - 0.6 * math.exp(-0.3 * layer)
        proj = _inproj(h, norm_w[layer], mod, layer, w_in)
        ga = _diffattn(proj, lam[layer], da_norm_w[layer], lam_init)
        gb = _retention(proj, ret_norm_w[layer])
        gc = _dilated(proj)
        h = _merge(ga, gb, gc, proj,
                   w_proj_a[layer].astype(BF16), w_proj_b[layer].astype(BF16), w_proj_c[layer].astype(BF16),
                   w_out[layer].astype(BF16), h, mod, layer, final_norm_w, final=layer == depth - 1)
    return h[None]
```

```python
import functools
import math

import numpy as np
import jax
import jax.numpy as jnp
from jax import lax
from jax.experimental import pallas as pl
from jax.experimental.pallas import tpu as pltpu

F32 = jnp.float32
BF16 = jnp.bfloat16

LANES = 128
VMEM_LIMIT_BYTES = 56 * 1024 * 1024

EPS = 1e-6
LOG2E = math.log2(math.e)

D_MODEL = 2048
DA_HEADS = 8
DA_QK_DIM = 64
DA_V_DIM = 128
RET_HEADS = 8
RET_QK_DIM = 64
RET_V_DIM = 128
RET_CHUNK = 256
ROT_BASE = 10000.0
DIL_DILATIONS = (1, 4, 16)
DIL_RADIUS = 128
DIL_HEADS = 8
DIL_HEAD_DIM = 128
DIL_TILE = 2048
DIL_UNROLL = 4
MASK_VALUE = -1e30

CB_AQ, CB_AK, CB_AV, CB_AZ = 0, 8, 16, 24
CB_BQ, CB_BK, CB_BV, CB_BZ = 32, 36, 40, 48
CB_CQ, CB_CK, CB_CV, CB_CZ = 56, 80, 104, 128
CB_GA, CB_GB, CB_GC = 136, 152, 168
GATE_CB = 8
IN_WIDTH = 184 * LANES


def _params(*sem):
    return pltpu.CompilerParams(dimension_semantics=sem, vmem_limit_bytes=VMEM_LIMIT_BYTES)


def _silu(z):
    return z * jax.nn.sigmoid(z)


def _adaln_body(c_ref, w_ref, b_ref, o_ref):
    cc = c_ref[...]
    o_ref[0] = jnp.sum(_silu(cc) * w_ref[0], axis=0, keepdims=True) + b_ref[0]


def _adaln(c, w_ada, b_ada):
    n_layers, d, e = w_ada.shape
    tn = 768
    return pl.pallas_call(
        _adaln_body,
        grid=(n_layers, e // tn),
        in_specs=[
            pl.BlockSpec((d, 1), lambda l, j: (0, 0)),
            pl.BlockSpec((1, d, tn), lambda l, j: (l, 0, j)),
            pl.BlockSpec((1, 1, tn), lambda l, j: (l, 0, j)),
        ],
        out_specs=pl.BlockSpec((1, 1, tn), lambda l, j: (l, 0, j)),
        out_shape=jax.ShapeDtypeStruct((n_layers, 1, e), F32),
        compiler_params=_params("arbitrary", "arbitrary"),
        name="adaln",
    )(c.reshape(d, 1), w_ada, b_ada.reshape(n_layers, 1, e))


def _inproj_body(x_ref, nw_ref, shift_ref, scale_ref, w_ref, o_ref, u_ref, *, tn, a_scale, c_scale):
    j = pl.program_id(1)

    @pl.when(j == 0)
    def _():
        xf = x_ref[...]
        y = xf * lax.rsqrt(jnp.mean(xf * xf, axis=-1, keepdims=True) + EPS) * nw_ref[...]
        u_ref[...] = (y * (1.0 + scale_ref[0]) + shift_ref[0]).astype(BF16)

    acc = jnp.dot(u_ref[...], w_ref[...].astype(BF16), preferred_element_type=F32)
    cb = j * (tn // LANES)
    qscale = jnp.where(cb < CB_AK, a_scale, jnp.where((cb >= CB_CQ) & (cb < CB_CK), c_scale, 1.0))
    acc = acc * qscale
    for cblk in range(tn // LANES):
        o_ref[cblk] = acc[:, cblk * LANES:(cblk + 1) * LANES].astype(BF16)


def _inproj(x, norm_w, mod, layer, w_in):
    s, d = x.shape
    tm, tn = 2048, 512
    body = functools.partial(
        _inproj_body, tn=tn,
        a_scale=DA_QK_DIM ** -0.5 * LOG2E, c_scale=DIL_HEAD_DIM ** -0.5 * LOG2E)
    return pl.pallas_call(
        body,
        grid=(s // tm, IN_WIDTH // tn),
        in_specs=[
            pl.BlockSpec((tm, d), lambda i, j: (i, 0), pipeline_mode=pl.Buffered(1)),
            pl.BlockSpec((1, d), lambda i, j: (0, 0)),
            pl.BlockSpec((1, 1, d), lambda i, j: (layer, 0, 0)),
            pl.BlockSpec((1, 1, d), lambda i, j: (layer, 0, 1)),
            pl.BlockSpec((None, d, tn), lambda i, j: (layer, 0, j)),
        ],
        out_specs=pl.BlockSpec((tn // LANES, tm, LANES), lambda i, j: (j, i, 0)),
        out_shape=jax.ShapeDtypeStruct((IN_WIDTH // LANES, s, LANES), BF16),
        scratch_shapes=[pltpu.VMEM((tm, d), BF16)],
        compiler_params=_params("arbitrary", "arbitrary"),
        name="inproj",
    )(x, norm_w.reshape(1, d), mod, mod, w_in)


DA_HEADS_PER_STEP = 2
DA_ONES_ROWS = 16


def _diffattn_body(lam_ref, nw_ref, q_ref, k_ref, v_ref, z_ref, o_ref, qs_ref, vt_ref, m_ref, acc_ref,
                   *, tq, lam_init):
    i = pl.program_id(1)
    nh = q_ref.shape[0]
    dv = DA_V_DIM

    @pl.when(i == 0)
    def _():
        ones = jnp.ones((DA_ONES_ROWS, tq), BF16)
        for hh in range(nh):
            def transpose_v(c, carry, hh=hh):
                start = pl.multiple_of(c * tq, tq)
                vt_ref[hh, c, 0:dv, :] = v_ref[hh, pl.ds(start, tq), :].astype(F32).T.astype(BF16)
                vt_ref[hh, c, dv:dv + DA_ONES_ROWS, :] = ones
                return carry

            lax.fori_loop(0, vt_ref.shape[1], transpose_v, 0)

    lane = lax.broadcasted_iota(jnp.int32, (tq, LANES), 1)
    for hh in range(nh):
        q = q_ref[hh]
        zero = jnp.zeros_like(q)
        qs_ref[hh, 0:tq] = jnp.where(lane < DA_QK_DIM, q, zero)
        qs_ref[hh, tq:2 * tq] = jnp.where(lane >= DA_QK_DIM, q, zero)
    m_ref[...] = jnp.full(m_ref.shape, -jnp.inf, F32)
    acc_ref[...] = jnp.zeros(acc_ref.shape, F32)

    def step(kt, diagonal):
        start = pl.multiple_of(kt * tq, tq)
        for hh in range(nh):
            k = k_ref[hh, pl.ds(start, tq), :]
            st = lax.dot_general(k, qs_ref[hh], (((1,), (1,)), ((), ())), preferred_element_type=F32)
            if diagonal:
                key = lax.broadcasted_iota(jnp.int32, st.shape, 0)
                col = lax.broadcasted_iota(jnp.int32, st.shape, 1)
                st = jnp.where(key <= jnp.where(col >= tq, col - tq, col), st, -jnp.inf)
            m_prev = m_ref[hh]
            m_new = jnp.maximum(m_prev, jnp.max(st, axis=0, keepdims=True))
            alpha = jnp.exp2(m_prev - m_new)
            pt = jnp.exp2(st - m_new).astype(BF16)
            acc_ref[hh] = alpha * acc_ref[hh] + jnp.dot(vt_ref[hh, kt], pt, preferred_element_type=F32)
            m_ref[hh] = m_new

    def off_diagonal(kt, carry):
        step(kt, False)
        return carry

    lax.fori_loop(0, i, off_diagonal, 0)
    step(i, True)

    lam = lam_ref[...]
    lam_full = (jnp.exp(jnp.sum(lam[0:1] * lam[1:2], axis=-1, keepdims=True))
                - jnp.exp(jnp.sum(lam[2:3] * lam[3:4], axis=-1, keepdims=True)) + lam_init)
    for hh in range(nh):
        acc = acc_ref[hh]
        inv = 1.0 / acc[dv:dv + 1, :]
        ot = acc[0:dv, 0:tq] * inv[:, 0:tq] - lam_full * (acc[0:dv, tq:2 * tq] * inv[:, tq:2 * tq])
        o = ot.T
        o = o * lax.rsqrt(jnp.mean(o * o, axis=-1, keepdims=True) + EPS) * nw_ref[...] * (1.0 - lam_init)
        o_ref[:, hh * dv:(hh + 1) * dv] = (o * _silu(z_ref[hh].astype(F32))).astype(BF16)


def _diffattn(proj, lam, norm_w, lam_init):
    s = proj.shape[1]
    tq = 512
    nh = DA_HEADS_PER_STEP
    body = functools.partial(_diffattn_body, tq=tq, lam_init=lam_init)
    return pl.pallas_call(
        body,
        grid=(DA_HEADS // nh, s // tq),
        in_specs=[
            pl.BlockSpec((4, DA_QK_DIM), lambda h, i: (0, 0)),
            pl.BlockSpec((1, DA_V_DIM), lambda h, i: (0, 0)),
            pl.BlockSpec((nh, tq, LANES), lambda h, i: (CB_AQ // nh + h, i, 0)),
            pl.BlockSpec((nh, s, LANES), lambda h, i: (CB_AK // nh + h, 0, 0)),
            pl.BlockSpec((nh, s, LANES), lambda h, i: (CB_AV // nh + h, 0, 0)),
            pl.BlockSpec((nh, tq, LANES), lambda h, i: (CB_AZ // nh + h, i, 0)),
        ],
        out_specs=pl.BlockSpec((tq, nh * DA_V_DIM), lambda h, i: (i, h)),
        out_shape=jax.ShapeDtypeStruct((s, DA_HEADS * DA_V_DIM), BF16),
        scratch_shapes=[
            pltpu.VMEM((nh, 2 * tq, LANES), BF16),
            pltpu.VMEM((nh, s // tq, DA_V_DIM + DA_ONES_ROWS, tq), BF16),
            pltpu.VMEM((nh, 1, 2 * tq), F32),
            pltpu.VMEM((nh, DA_V_DIM + DA_ONES_ROWS, 2 * tq), F32),
        ],
        compiler_params=_params("arbitrary", "arbitrary"),
        name="diffattn",
    )(lam, norm_w.reshape(1, DA_V_DIM), proj, proj, proj, proj)


def _retention_tables(s):
    c = RET_CHUNK
    half = RET_QK_DIM // 2
    theta = 1.0 / (ROT_BASE ** jnp.linspace(0.0, 1.0, half, dtype=F32))
    ang = jnp.arange(s, dtype=F32)[:, None] * theta[None, :]
    cos = jnp.tile(jnp.repeat(jnp.cos(ang), 2, axis=-1), (1, 2))
    sign = jnp.tile(jnp.asarray([-1.0, 1.0], F32), RET_QK_DIM)
    sin = jnp.tile(jnp.repeat(jnp.sin(ang), 2, axis=-1), (1, 2)) * sign[None, :]
    log_g = jnp.log1p(-jnp.exp2(-5.0 - jnp.arange(RET_HEADS, dtype=F32)))
    idx = jnp.arange(c, dtype=F32)
    lane_g = jnp.repeat(log_g.reshape(RET_HEADS // 2, 2), RET_QK_DIM, axis=-1)
    qdec = jnp.exp((idx + 1.0)[None, :, None] * lane_g[:, None, :])
    kdec = jnp.exp((c - 1.0 - idx)[None, :, None] * lane_g[:, None, :]) * RET_QK_DIM ** -0.5
    cdec = jnp.broadcast_to(jnp.exp(c * lane_g)[:, :, None], (RET_HEADS // 2, LANES, RET_V_DIM))
    causal = idx[:, None] >= idx[None, :]
    dmask = jnp.where(causal[None], jnp.exp(-c * log_g)[:, None, None], 0.0)
    return cos, sin, qdec, kdec, cdec, dmask.astype(F32)


def _retention_body(cos_ref, sin_ref, qdec_ref, kdec_ref, cdec_ref, dmask_ref, nw_ref,
                    q_ref, k_ref, v_ref, z_ref, o_ref, r_ref):
    @pl.when(pl.program_id(1) == 0)
    def _():
        r_ref[...] = jnp.zeros(r_ref.shape, F32)

    c = q_ref.shape[1]
    cos = cos_ref[...]
    sin = sin_ref[...]
    lane = lax.broadcasted_iota(jnp.int32, (c, LANES), 1)
    even = (lane & 1) == 0
    head0 = lane < RET_QK_DIM

    def rotate(x):
        nxt = pltpu.roll(x, LANES - 1, 1)
        prv = pltpu.roll(x, 1, 1)
        return x * cos + jnp.where(even, nxt, prv) * sin

    qd = (rotate(q_ref[0].astype(F32)) * qdec_ref[0]).astype(BF16)
    kd = (rotate(k_ref[0].astype(F32)) * kdec_ref[0]).astype(BF16)
    zero = jnp.zeros_like(qd)
    state = r_ref[...]
    state_b = state.astype(BF16)
    updates = []
    for hh in range(2):
        qz = jnp.where(head0, qd, zero) if hh == 0 else jnp.where(head0, zero, qd)
        sc = lax.dot_general(qz, kd, (((1,), (1,)), ((), ())), preferred_element_type=F32) * dmask_ref[hh]
        v = v_ref[hh]
        o = (jnp.dot(sc.astype(BF16), v, preferred_element_type=F32)
             + jnp.dot(qz, state_b, preferred_element_type=F32))
        o = o * lax.rsqrt(jnp.mean(o * o, axis=-1, keepdims=True) + EPS) * nw_ref[...]
        o_ref[:, hh * RET_V_DIM:(hh + 1) * RET_V_DIM] = (o * _silu(z_ref[hh].astype(F32))).astype(BF16)
        updates.append(lax.dot_general(kd, v, (((0,), (0,)), ((), ())), preferred_element_type=F32))
    row = lax.broadcasted_iota(jnp.int32, state.shape, 0)
    r_ref[...] = cdec_ref[0] * state + jnp.where(row < RET_QK_DIM, updates[0], updates[1])


def _retention(proj, norm_w):
    s = proj.shape[1]
    c = RET_CHUNK
    pairs = RET_HEADS // 2
    cos, sin, qdec, kdec, cdec, dmask = _retention_tables(s)
    return pl.pallas_call(
        _retention_body,
        grid=(pairs, s // c),
        in_specs=[
            pl.BlockSpec((c, LANES), lambda p, n: (n, 0)),
            pl.BlockSpec((c, LANES), lambda p, n: (n, 0)),
            pl.BlockSpec((1, c, LANES), lambda p, n: (p, 0, 0)),
            pl.BlockSpec((1, c, LANES), lambda p, n: (p, 0, 0)),
            pl.BlockSpec((1, LANES, RET_V_DIM), lambda p, n: (p, 0, 0)),
            pl.BlockSpec((2, c, c), lambda p, n: (p, 0, 0)),
            pl.BlockSpec((1, RET_V_DIM), lambda p, n: (0, 0)),
            pl.BlockSpec((1, c, LANES), lambda p, n: (CB_BQ + p, n, 0)),
            pl.BlockSpec((1, c, LANES), lambda p, n: (CB_BK + p, n, 0)),
            pl.BlockSpec((2, c, LANES), lambda p, n: (CB_BV // 2 + p, n, 0)),
            pl.BlockSpec((2, c, LANES), lambda p, n: (CB_BZ // 2 + p, n, 0)),
        ],
        out_specs=pl.BlockSpec((c, 2 * RET_V_DIM), lambda p, n: (n, p)),
        out_shape=jax.ShapeDtypeStruct((s, RET_HEADS * RET_V_DIM), BF16),
        scratch_shapes=[pltpu.VMEM((LANES, RET_V_DIM), F32)],
        compiler_params=_params("arbitrary", "arbitrary"),
        name="retention",
    )(cos, sin, qdec, kdec, cdec, dmask, norm_w.reshape(1, RET_V_DIM), proj, proj, proj, proj)


def _dilated_body(*refs):
    (q0, k0, v0, q1, k1, v1, q2, k2, v2, z_ref, o_ref,
     slab, qp, kp0, vp0, kp1, vp1, kp2, vp2, og0, og1, og2, ls0, ls1, ls2) = refs
    t = pl.program_id(1)
    blk = DIL_RADIUS
    groups = (
        (1, q0, k0, v0, kp0, vp0, og0, ls0),
        (4, q1, k1, v1, kp1, vp1, og1, ls1),
        (16, q2, k2, v2, kp2, vp2, og2, ls2),
    )

    qi = lax.broadcasted_iota(jnp.int32, (blk, 2 * blk), 0)
    kj = lax.broadcasted_iota(jnp.int32, (blk, 2 * blk), 1)
    in_window = (kj >= qi) & (kj <= qi + blk)
    bias = jnp.where(in_window, 0.0, MASK_VALUE).astype(F32)
    bias_first = jnp.where(in_window & (kj >= blk), 0.0, MASK_VALUE).astype(F32)

    for d, q_ref, k_ref, v_ref, kp, vp, og, ls in groups:
        m_t = DIL_TILE // d
        nb = m_t // blk

        @pl.when(t == 0)
        def _():
            kp[:, 0:blk, :] = jnp.zeros((d, blk, LANES), BF16)
            vp[:, 0:blk, :] = jnp.zeros((d, blk, LANES), BF16)

        if d == 1:
            qp[...] = q_ref[0]
            kp[0, blk:blk + m_t, :] = k_ref[0]
            vp[0, blk:blk + m_t, :] = v_ref[0]
        else:
            slab[...] = q_ref[0].astype(F32)
            for r in range(d):
                qp[r * m_t:(r + 1) * m_t, :] = slab[pl.ds(r, m_t, stride=d), :].astype(BF16)
            slab[...] = k_ref[0].astype(F32)
            for r in range(d):
                kp[r, blk:blk + m_t, :] = slab[pl.ds(r, m_t, stride=d), :].astype(BF16)
            slab[...] = v_ref[0].astype(F32)
            for r in range(d):
                vp[r, blk:blk + m_t, :] = slab[pl.ds(r, m_t, stride=d), :].astype(BF16)

        def unit(u, carry, d=d, m_t=m_t, nb=nb, kp=kp, vp=vp, og=og, ls=ls):
            r = u // nb
            b = u % nb
            qu = qp[pl.ds(pl.multiple_of(r * m_t + b * blk, blk), blk), :]
            start = pl.multiple_of(b * blk, blk)
            kw = kp[r, pl.ds(start, 2 * blk), :]
            vw = vp[r, pl.ds(start, 2 * blk), :]
            s = lax.dot_general(qu, kw, (((1,), (1,)), ((), ())), preferred_element_type=F32)
            s = s + jnp.where((t == 0) & (b == 0), bias_first, bias)
            m = jnp.max(s, axis=-1, keepdims=True)
            p = jnp.exp2(s - m)
            l = jnp.sum(p, axis=-1, keepdims=True)
            o = jnp.dot(p.astype(BF16), vw, preferred_element_type=F32) / l
            lse = jnp.broadcast_to(m + jnp.log2(l), (blk, LANES))
            if d == 1:
                row0 = pl.multiple_of(b * blk, blk)
                og[pl.ds(row0, blk), :] = o
                ls[pl.ds(row0, blk), :] = lse
            else:
                row0 = b * (blk * d) + r
                og[pl.ds(row0, blk, stride=d), :] = o
                ls[pl.ds(row0, blk, stride=d), :] = lse
            return carry

        lax.fori_loop(0, d * nb, unit, 0, unroll=DIL_UNROLL)

        kp[:, 0:blk, :] = kp[:, m_t:m_t + blk, :]
        vp[:, 0:blk, :] = vp[:, m_t:m_t + blk, :]

    l0, l1, l2 = ls0[...], ls1[...], ls2[...]
    mx = jnp.maximum(jnp.maximum(l0, l1), l2)
    w0, w1, w2 = jnp.exp2(l0 - mx), jnp.exp2(l1 - mx), jnp.exp2(l2 - mx)
    o = (w0 * og0[...] + w1 * og1[...] + w2 * og2[...]) / (w0 + w1 + w2)
    o_ref[...] = (o * _silu(z_ref[0].astype(F32))).astype(BF16)


def _dilated(proj):
    s = proj.shape[1]
    tile = DIL_TILE
    blk = DIL_RADIUS

    def qkv_spec(base, g):
        return pl.BlockSpec((1, tile, LANES), lambda h, t: (base + g * DIL_HEADS + h, t, 0))

    in_specs = []
    for g in range(len(DIL_DILATIONS)):
        in_specs += [qkv_spec(CB_CQ, g), qkv_spec(CB_CK, g), qkv_spec(CB_CV, g)]
    in_specs.append(pl.BlockSpec((1, tile, LANES), lambda h, t: (CB_CZ + h, t, 0)))
    scratch = [pltpu.VMEM((tile, LANES), F32), pltpu.VMEM((tile, LANES), BF16)]
    for d in DIL_DILATIONS:
        scratch += [pltpu.VMEM((d, blk + tile // d, LANES), BF16)] * 2
    scratch += [pltpu.VMEM((tile, LANES), F32)] * 6
    return pl.pallas_call(
        _dilated_body,
        grid=(DIL_HEADS, s // tile),
        in_specs=in_specs,
        out_specs=pl.BlockSpec((tile, DIL_HEAD_DIM), lambda h, t: (t, h)),
        out_shape=jax.ShapeDtypeStruct((s, DIL_HEADS * DIL_HEAD_DIM), BF16),
        scratch_shapes=scratch,
        compiler_params=_params("arbitrary", "arbitrary"),
        name="dilated",
    )(*([proj] * 10))


def _merge_body(*refs, final):
    (ga_ref, gb_ref, gc_ref, ta0, ta1, tb0, tb1, tc0, tc1, pa_ref, pb_ref, pc_ref, wo_ref,
     h_ref, gate_ref, fw_ref, o_ref) = refs

    def branch(g_ref, t_refs, p_ref):
        y = jnp.dot(g_ref[...], p_ref[...], preferred_element_type=F32)
        gate = jnp.concatenate([t[cblk] for t in t_refs for cblk in range(t.shape[0])], axis=-1).astype(F32)
        return jax.nn.sigmoid(gate) * y

    merged = (branch(ga_ref, (ta0, ta1), pa_ref) + branch(gb_ref, (tb0, tb1), pb_ref)
              + branch(gc_ref, (tc0, tc1), pc_ref))
    y = jnp.dot(merged.astype(BF16), wo_ref[...], preferred_element_type=F32)
    out = h_ref[...] + gate_ref[0] * y
    if final:
        out = out * lax.rsqrt(jnp.mean(out * out, axis=-1, keepdims=True) + EPS) * fw_ref[...]
    o_ref[...] = out


def _merge(ga, gb, gc, proj, pa, pb, pc, wo, h, mod, layer, final_w, final):
    s, d = h.shape
    w = ga.shape[1]
    tm = 256
    const = dict(pipeline_mode=pl.Buffered(1))
    return pl.pallas_call(
        functools.partial(_merge_body, final=final),
        grid=(s // tm,),
        in_specs=[
            pl.BlockSpec((tm, w), lambda i: (i, 0)),
            pl.BlockSpec((tm, w), lambda i: (i, 0)),
            pl.BlockSpec((tm, w), lambda i: (i, 0)),
            *[pl.BlockSpec((GATE_CB, tm, LANES), functools.partial(lambda i, b: (b, i, 0), b=CB_GA // GATE_CB + b))
              for b in range(3 * d // LANES // GATE_CB)],
            pl.BlockSpec((w, d), lambda i: (0, 0), **const),
            pl.BlockSpec((w, d), lambda i: (0, 0), **const),
            pl.BlockSpec((w, d), lambda i: (0, 0), **const),
            pl.BlockSpec((d, d), lambda i: (0, 0), **const),
            pl.BlockSpec((tm, d), lambda i: (i, 0)),
            pl.BlockSpec((1, 1, d), lambda i: (layer, 0, 2)),
            pl.BlockSpec((1, d), lambda i: (0, 0)),
        ],
        out_specs=pl.BlockSpec((tm, d), lambda i: (i, 0)),
        out_shape=jax.ShapeDtypeStruct((s, d), F32),
        compiler_params=_params("arbitrary"),
        name="merge",
    )(ga, gb, gc, *([proj] * 6), pa, pb, pc, wo, h, mod, final_w.reshape(1, d))


def kernel(x, c, w_in, w_proj_a, w_proj_b, w_proj_c, w_out, w_ada, b_ada, norm_w, lam, da_norm_w, ret_norm_w,
           final_norm_w):
    batch, s, d = x.shape
    assert batch == 1 and d == D_MODEL and s % DIL_TILE == 0 and w_in.shape[2] == IN_WIDTH
    depth = w_in.shape[0]
    mod = _adaln(c, w_ada, b_ada)
    h = x[0]
    for layer in range(depth):
        lam_init = 0.8 - 0.6 * math.exp(-0.3 * layer)
        proj = _inproj(h, norm_w[layer], mod, layer, w_in)
        ga = _diffattn(proj, lam[layer], da_norm_w[layer], lam_init)
        gb = _retention(proj, ret_norm_w[layer])
        gc = _dilated(proj)
        h = _merge(ga, gb, gc, proj,
                   w_proj_a[layer].astype(BF16), w_proj_b[layer].astype(BF16), w_proj_c[layer].astype(BF16),
                   w_out[layer].astype(BF16), h, mod, layer, final_norm_w, final=layer == depth - 1)
    return h[None]
```

```python
import functools
import math

import numpy as np
import jax
import jax.numpy as jnp
from jax import lax
from jax.experimental import pallas as pl
from jax.experimental.pallas import tpu as pltpu

F32 = jnp.float32
BF16 = jnp.bfloat16

LANES = 128
VMEM_LIMIT_BYTES = 56 * 1024 * 1024

EPS = 1e-6
LOG2E = math.log2(math.e)

D_MODEL = 2048
DA_HEADS = 8
DA_QK_DIM = 64
DA_V_DIM = 128
RET_HEADS = 8
RET_QK_DIM = 64
RET_V_DIM = 128
RET_CHUNK = 256
ROT_BASE = 10000.0
DIL_DILATIONS = (1, 4, 16)
DIL_RADIUS = 128
DIL_HEADS = 8
DIL_HEAD_DIM = 128
DIL_TILE = 2048
DIL_UNROLL = 16
MASK_VALUE = -1e30

CB_AQ, CB_AK, CB_AV, CB_AZ = 0, 8, 16, 24
CB_BQ, CB_BK, CB_BV, CB_BZ = 32, 36, 40, 48
CB_CQ, CB_CK, CB_CV, CB_CZ = 56, 80, 104, 128
CB_GA, CB_GB, CB_GC = 136, 152, 168
GATE_CB = 8
IN_WIDTH = 184 * LANES


def _params(*sem, flags=None):
    return pltpu.CompilerParams(dimension_semantics=sem, vmem_limit_bytes=VMEM_LIMIT_BYTES, flags=flags)


def _silu(z):
    return z * jax.nn.sigmoid(z)


def _adaln_body(c_ref, w_ref, b_ref, o_ref):
    cc = c_ref[...]
    o_ref[0] = jnp.sum(_silu(cc) * w_ref[0], axis=0, keepdims=True) + b_ref[0]


def _adaln(c, w_ada, b_ada):
    n_layers, d, e = w_ada.shape
    tn = 768
    return pl.pallas_call(
        _adaln_body,
        grid=(n_layers, e // tn),
        in_specs=[
            pl.BlockSpec((d, 1), lambda l, j: (0, 0)),
            pl.BlockSpec((1, d, tn), lambda l, j: (l, 0, j)),
            pl.BlockSpec((1, 1, tn), lambda l, j: (l, 0, j)),
        ],
        out_specs=pl.BlockSpec((1, 1, tn), lambda l, j: (l, 0, j)),
        out_shape=jax.ShapeDtypeStruct((n_layers, 1, e), F32),
        compiler_params=_params("arbitrary", "arbitrary"),
        name="adaln",
    )(c.reshape(d, 1), w_ada, b_ada.reshape(n_layers, 1, e))


def _inproj_body(x_ref, nw_ref, shift_ref, scale_ref, w_ref, o_ref, u_ref, *, tn, a_scale, c_scale):
    j = pl.program_id(1)

    @pl.when(j == 0)
    def _():
        xf = x_ref[...]
        y = xf * lax.rsqrt(jnp.mean(xf * xf, axis=-1, keepdims=True) + EPS) * nw_ref[...]
        u_ref[...] = (y * (1.0 + scale_ref[0]) + shift_ref[0]).astype(BF16)

    acc = jnp.dot(u_ref[...], w_ref[...].astype(BF16), preferred_element_type=F32)
    cb = j * (tn // LANES)
    qscale = jnp.where(cb < CB_AK, a_scale, jnp.where((cb >= CB_CQ) & (cb < CB_CK), c_scale, 1.0))
    acc = acc * qscale
    for cblk in range(tn // LANES):
        o_ref[cblk] = acc[:, cblk * LANES:(cblk + 1) * LANES].astype(BF16)


def _inproj(x, norm_w, mod, layer, w_in):
    s, d = x.shape
    tm, tn = 2048, 512
    body = functools.partial(
        _inproj_body, tn=tn,
        a_scale=DA_QK_DIM ** -0.5 * LOG2E, c_scale=DIL_HEAD_DIM ** -0.5 * LOG2E)
    return pl.pallas_call(
        body,
        grid=(s // tm, IN_WIDTH // tn),
        in_specs=[
            pl.BlockSpec((tm, d), lambda i, j: (i, 0), pipeline_mode=pl.Buffered(1)),
            pl.BlockSpec((1, d), lambda i, j: (0, 0)),
            pl.BlockSpec((1, 1, d), lambda i, j: (layer, 0, 0)),
            pl.BlockSpec((1, 1, d), lambda i, j: (layer, 0, 1)),
            pl.BlockSpec((None, d, tn), lambda i, j: (layer, 0, j)),
        ],
        out_specs=pl.BlockSpec((tn // LANES, tm, LANES), lambda i, j: (j, i, 0)),
        out_shape=jax.ShapeDtypeStruct((IN_WIDTH // LANES, s, LANES), BF16),
        scratch_shapes=[pltpu.VMEM((tm, d), BF16)],
        compiler_params=_params("arbitrary", "arbitrary"),
        name="inproj",
    )(x, norm_w.reshape(1, d), mod, mod, w_in)


DA_HEADS_PER_STEP = 4
DA_QCHUNK = 256
DA_ONES_ROWS = 16


def _diffattn_body(lam_ref, nw_ref, q_ref, k_ref, v_ref, z_ref, o_ref, qs_ref, vt_ref, m_ref, acc_ref,
                   *, tq, lam_init):
    i = pl.program_id(1)
    nh = q_ref.shape[0]
    dv = DA_V_DIM

    @pl.when(i == 0)
    def _():
        ones = jnp.ones((DA_ONES_ROWS, tq), BF16)
        for hh in range(nh):
            def transpose_v(c, carry, hh=hh):
                start = pl.multiple_of(c * tq, tq)
                vt_ref[hh, c, 0:dv, :] = v_ref[hh, pl.ds(start, tq), :].astype(F32).T.astype(BF16)
                vt_ref[hh, c, dv:dv + DA_ONES_ROWS, :] = ones
                return carry

            lax.fori_loop(0, vt_ref.shape[1], transpose_v, 0)

    lane = lax.broadcasted_iota(jnp.int32, (tq, LANES), 1)
    for hh in range(nh):
        q = q_ref[hh]
        zero = jnp.zeros_like(q)
        qs_ref[hh, 0:tq] = jnp.where(lane < DA_QK_DIM, q, zero)
        qs_ref[hh, tq:2 * tq] = jnp.where(lane >= DA_QK_DIM, q, zero)
    m_ref[...] = jnp.full(m_ref.shape, -jnp.inf, F32)
    acc_ref[...] = jnp.zeros(acc_ref.shape, F32)

    def step(kt, diagonal):
        start = pl.multiple_of(kt * tq, tq)
        chains = [(hh, c * DA_QCHUNK) for hh in range(nh) for c in range(2 * tq // DA_QCHUNK)]
        def qk(hh, c0):
            k = k_ref[hh, pl.ds(start, tq), :]
            st = lax.dot_general(k, qs_ref[hh, c0:c0 + DA_QCHUNK, :], (((1,), (1,)), ((), ())),
                                 preferred_element_type=F32)
            if diagonal:
                key = lax.broadcasted_iota(jnp.int32, st.shape, 0)
                qpos = lax.broadcasted_iota(jnp.int32, st.shape, 1) + (c0 % tq)
                st = jnp.where(key <= qpos, st, -jnp.inf)
            return st

        def softmax(hh, c0, st):
            m_prev = m_ref[hh, :, c0:c0 + DA_QCHUNK]
            m_new = jnp.maximum(m_prev, jnp.max(st, axis=0, keepdims=True))
            m_ref[hh, :, c0:c0 + DA_QCHUNK] = m_new
            return jnp.exp2(m_prev - m_new), jnp.exp2(st - m_new).astype(BF16)

        def pv(hh, c0, alpha, pt):
            acc_ref[hh, :, c0:c0 + DA_QCHUNK] = (alpha * acc_ref[hh, :, c0:c0 + DA_QCHUNK]
                                                 + jnp.dot(vt_ref[hh, kt], pt, preferred_element_type=F32))

        scores = [qk(*ch) for ch in chains]
        probs = [softmax(*ch, st) for ch, st in zip(chains, scores)]
        for ch, (alpha, pt) in zip(chains, probs):
            pv(*ch, alpha, pt)

    def off_diagonal(kt, carry):
        step(kt, False)
        return carry

    lax.fori_loop(0, i, off_diagonal, 0)
    step(i, True)

    lam = lam_ref[...]
    lam_full = (jnp.exp(jnp.sum(lam[0:1] * lam[1:2], axis=-1, keepdims=True))
                - jnp.exp(jnp.sum(lam[2:3] * lam[3:4], axis=-1, keepdims=True)) + lam_init)
    for hh in range(nh):
        acc = acc_ref[hh]
        inv = 1.0 / acc[dv:dv + 1, :]
        ot = acc[0:dv, 0:tq] * inv[:, 0:tq] - lam_full * (acc[0:dv, tq:2 * tq] * inv[:, tq:2 * tq])
        o = ot.T
        o = o * lax.rsqrt(jnp.mean(o * o, axis=-1, keepdims=True) + EPS) * nw_ref[...] * (1.0 - lam_init)
        o_ref[:, hh * dv:(hh + 1) * dv] = (o * _silu(z_ref[hh].astype(F32))).astype(BF16)


def _diffattn(proj, lam, norm_w, lam_init):
    s = proj.shape[1]
    tq = 512
    nh = DA_HEADS_PER_STEP
    body = functools.partial(_diffattn_body, tq=tq, lam_init=lam_init)
    return pl.pallas_call(
        body,
        grid=(DA_HEADS // nh, s // tq),
        in_specs=[
            pl.BlockSpec((4, DA_QK_DIM), lambda h, i: (0, 0)),
            pl.BlockSpec((1, DA_V_DIM), lambda h, i: (0, 0)),
            pl.BlockSpec((nh, tq, LANES), lambda h, i: (CB_AQ // nh + h, i, 0)),
            pl.BlockSpec((nh, s, LANES), lambda h, i: (CB_AK // nh + h, 0, 0), pipeline_mode=pl.Buffered(1)),
            pl.BlockSpec((nh, s, LANES), lambda h, i: (CB_AV // nh + h, 0, 0), pipeline_mode=pl.Buffered(1)),
            pl.BlockSpec((nh, tq, LANES), lambda h, i: (CB_AZ // nh + h, i, 0)),
        ],
        out_specs=pl.BlockSpec((tq, nh * DA_V_DIM), lambda h, i: (i, h)),
        out_shape=jax.ShapeDtypeStruct((s, DA_HEADS * DA_V_DIM), BF16),
        scratch_shapes=[
            pltpu.VMEM((nh, 2 * tq, LANES), BF16),
            pltpu.VMEM((nh, s // tq, DA_V_DIM + DA_ONES_ROWS, tq), BF16),
            pltpu.VMEM((nh, 1, 2 * tq), F32),
            pltpu.VMEM((nh, DA_V_DIM + DA_ONES_ROWS, 2 * tq), F32),
        ],
        compiler_params=_params("arbitrary", "arbitrary"),
        name="diffattn",
    )(lam, norm_w.reshape(1, DA_V_DIM), proj, proj, proj, proj)


def _retention_tables(s):
    c = RET_CHUNK
    half = RET_QK_DIM // 2
    theta = 1.0 / (ROT_BASE ** jnp.linspace(0.0, 1.0, half, dtype=F32))
    ang = jnp.arange(s, dtype=F32)[:, None] * theta[None, :]
    cos = jnp.tile(jnp.repeat(jnp.cos(ang), 2, axis=-1), (1, 2))
    sign = jnp.tile(jnp.asarray([-1.0, 1.0], F32), RET_QK_DIM)
    sin = jnp.tile(jnp.repeat(jnp.sin(ang), 2, axis=-1), (1, 2)) * sign[None, :]
    log_g = jnp.log1p(-jnp.exp2(-5.0 - jnp.arange(RET_HEADS, dtype=F32)))
    idx = jnp.arange(c, dtype=F32)
    lane_g = jnp.repeat(log_g.reshape(RET_HEADS // 2, 2), RET_QK_DIM, axis=-1)
    qdec = jnp.exp((idx + 1.0)[None, :, None] * lane_g[:, None, :])
    kdec = jnp.exp((c - 1.0 - idx)[None, :, None] * lane_g[:, None, :]) * RET_QK_DIM ** -0.5
    cdec = jnp.broadcast_to(jnp.exp(c * lane_g)[:, :, None], (RET_HEADS // 2, LANES, RET_V_DIM))
    causal = idx[:, None] >= idx[None, :]
    dmask = jnp.where(causal[None], jnp.exp(-c * log_g)[:, None, None], 0.0)
    return cos, sin, qdec, kdec, cdec, dmask.astype(F32)


def _retention_body(cos_ref, sin_ref, qdec_ref, kdec_ref, cdec_ref, dmask_ref, nw_ref,
                    q_ref, k_ref, v_ref, z_ref, o_ref, r_ref):
    @pl.when(pl.program_id(0) == 0)
    def _():
        r_ref[...] = jnp.zeros(r_ref.shape, F32)

    c = q_ref.shape[1]
    cos = cos_ref[...]
    sin = sin_ref[...]
    lane = lax.broadcasted_iota(jnp.int32, (c, LANES), 1)
    even = (lane & 1) == 0
    head0 = lane < RET_QK_DIM

    def rotate(x):
        nxt = pltpu.roll(x, LANES - 1, 1)
        prv = pltpu.roll(x, 1, 1)
        return x * cos + jnp.where(even, nxt, prv) * sin

    row = lax.broadcasted_iota(jnp.int32, r_ref.shape[1:], 0)
    for pp in range(q_ref.shape[0]):
        qd = (rotate(q_ref[pp].astype(F32)) * qdec_ref[pp]).astype(BF16)
        kd = (rotate(k_ref[pp].astype(F32)) * kdec_ref[pp]).astype(BF16)
        zero = jnp.zeros_like(qd)
        state = r_ref[pp]
        state_b = state.astype(BF16)
        updates = []
        for hh in range(2):
            head = 2 * pp + hh
            qz = jnp.where(head0, qd, zero) if hh == 0 else jnp.where(head0, zero, qd)
            sc = lax.dot_general(qz, kd, (((1,), (1,)), ((), ())), preferred_element_type=F32) * dmask_ref[head]
            v = v_ref[head]
            o = (jnp.dot(sc.astype(BF16), v, preferred_element_type=F32)
                 + jnp.dot(qz, state_b, preferred_element_type=F32))
            o = o * lax.rsqrt(jnp.mean(o * o, axis=-1, keepdims=True) + EPS) * nw_ref[...]
            o_ref[:, head * RET_V_DIM:(head + 1) * RET_V_DIM] = (o * _silu(z_ref[head].astype(F32))).astype(BF16)
            updates.append(lax.dot_general(kd, v, (((0,), (0,)), ((), ())), preferred_element_type=F32))
        r_ref[pp] = cdec_ref[pp] * state + jnp.where(row < RET_QK_DIM, updates[0], updates[1])


def _retention(proj, norm_w):
    s = proj.shape[1]
    c = RET_CHUNK
    pairs = RET_HEADS // 2
    cos, sin, qdec, kdec, cdec, dmask = _retention_tables(s)
    return pl.pallas_call(
        _retention_body,
        grid=(s // c,),
        in_specs=[
            pl.BlockSpec((c, LANES), lambda n: (n, 0)),
            pl.BlockSpec((c, LANES), lambda n: (n, 0)),
            pl.BlockSpec((pairs, c, LANES), lambda n: (0, 0, 0)),
            pl.BlockSpec((pairs, c, LANES), lambda n: (0, 0, 0)),
            pl.BlockSpec((pairs, LANES, RET_V_DIM), lambda n: (0, 0, 0)),
            pl.BlockSpec((RET_HEADS, c, c), lambda n: (0, 0, 0)),
            pl.BlockSpec((1, RET_V_DIM), lambda n: (0, 0)),
            pl.BlockSpec((pairs, c, LANES), lambda n: (CB_BQ // pairs, n, 0)),
            pl.BlockSpec((pairs, c, LANES), lambda n: (CB_BK // pairs, n, 0)),
            pl.BlockSpec((RET_HEADS, c, LANES), lambda n: (CB_BV // RET_HEADS, n, 0)),
            pl.BlockSpec((RET_HEADS, c, LANES), lambda n: (CB_BZ // RET_HEADS, n, 0)),
        ],
        out_specs=pl.BlockSpec((c, RET_HEADS * RET_V_DIM), lambda n: (n, 0)),
        out_shape=jax.ShapeDtypeStruct((s, RET_HEADS * RET_V_DIM), BF16),
        scratch_shapes=[pltpu.VMEM((pairs, LANES, RET_V_DIM), F32)],
        compiler_params=_params("arbitrary"),
        name="retention",
    )(cos, sin, qdec, kdec, cdec, dmask, norm_w.reshape(1, RET_V_DIM), proj, proj, proj, proj)


def _dilated_body(*refs):
    (q0, k0, v0, q1, k1, v1, q2, k2, v2, z_ref, o_ref,
     slab, qp, kp0, vp0, kp1, vp1, kp2, vp2, og0, og1, og2, ls0, ls1, ls2) = refs
    t = pl.program_id(1)
    blk = DIL_RADIUS
    groups = (
        (1, q0, k0, v0, kp0, vp0, og0, ls0),
        (4, q1, k1, v1, kp1, vp1, og1, ls1),
        (16, q2, k2, v2, kp2, vp2, og2, ls2),
    )

    qi = lax.broadcasted_iota(jnp.int32, (blk, 2 * blk), 0)
    kj = lax.broadcasted_iota(jnp.int32, (blk, 2 * blk), 1)
    in_window = (kj >= qi) & (kj <= qi + blk)
    bias = jnp.where(in_window, 0.0, MASK_VALUE).astype(F32)
    bias_first = jnp.where(in_window & (kj >= blk), 0.0, MASK_VALUE).astype(F32)

    for d, q_ref, k_ref, v_ref, kp, vp, og, ls in groups:
        m_t = DIL_TILE // d
        nb = m_t // blk

        @pl.when(t == 0)
        def _():
            kp[:, 0:blk, :] = jnp.zeros((d, blk, LANES), BF16)
            vp[:, 0:blk, :] = jnp.zeros((d, blk, LANES), BF16)

        if d == 1:
            qp[...] = q_ref[0]
            kp[0, blk:blk + m_t, :] = k_ref[0]
            vp[0, blk:blk + m_t, :] = v_ref[0]
        else:
            slab[...] = q_ref[0].astype(F32)
            for r in range(d):
                qp[r * m_t:(r + 1) * m_t, :] = slab[pl.ds(r, m_t, stride=d), :].astype(BF16)
            slab[...] = k_ref[0].astype(F32)
            for r in range(d):
                kp[r, blk:blk + m_t, :] = slab[pl.ds(r, m_t, stride=d), :].astype(BF16)
            slab[...] = v_ref[0].astype(F32)
            for r in range(d):
                vp[r, blk:blk + m_t, :] = slab[pl.ds(r, m_t, stride=d), :].astype(BF16)

        def unit(u, carry, d=d, m_t=m_t, nb=nb, kp=kp, vp=vp, og=og, ls=ls):
            r = u // nb
            b = u % nb
            qu = qp[pl.ds(pl.multiple_of(r * m_t + b * blk, blk), blk), :]
            start = pl.multiple_of(b * blk, blk)
            kw = kp[r, pl.ds(start, 2 * blk), :]
            vw = vp[r, pl.ds(start, 2 * blk), :]
            s = lax.dot_general(qu, kw, (((1,), (1,)), ((), ())), preferred_element_type=F32)
            s = s + jnp.where((t == 0) & (b == 0), bias_first, bias)
            m = jnp.max(s, axis=-1, keepdims=True)
            p = jnp.exp2(s - m)
            l = jnp.sum(p, axis=-1, keepdims=True)
            o = jnp.dot(p.astype(BF16), vw, preferred_element_type=F32) / l
            lse = jnp.broadcast_to(m + jnp.log2(l), (blk, LANES))
            if d == 1:
                row0 = pl.multiple_of(b * blk, blk)
                og[pl.ds(row0, blk), :] = o
                ls[pl.ds(row0, blk), :] = lse
            else:
                row0 = b * (blk * d) + r
                og[pl.ds(row0, blk, stride=d), :] = o
                ls[pl.ds(row0, blk, stride=d), :] = lse
            return carry

        lax.fori_loop(0, d * nb, unit, 0, unroll=DIL_UNROLL)

        kp[:, 0:blk, :] = kp[:, m_t:m_t + blk, :]
        vp[:, 0:blk, :] = vp[:, m_t:m_t + blk, :]

    l0, l1, l2 = ls0[...], ls1[...], ls2[...]
    mx = jnp.maximum(jnp.maximum(l0, l1), l2)
    w0, w1, w2 = jnp.exp2(l0 - mx), jnp.exp2(l1 - mx), jnp.exp2(l2 - mx)
    o = (w0 * og0[...] + w1 * og1[...] + w2 * og2[...]) / (w0 + w1 + w2)
    o_ref[...] = (o * _silu(z_ref[0].astype(F32))).astype(BF16)


def _dilated(proj):
    s = proj.shape[1]
    tile = DIL_TILE
    blk = DIL_RADIUS

    def qkv_spec(base, g):
        return pl.BlockSpec((1, tile, LANES), lambda h, t: (base + g * DIL_HEADS + h, t, 0))

    in_specs = []
    for g in range(len(DIL_DILATIONS)):
        in_specs += [qkv_spec(CB_CQ, g), qkv_spec(CB_CK, g), qkv_spec(CB_CV, g)]
    in_specs.append(pl.BlockSpec((1, tile, LANES), lambda h, t: (CB_CZ + h, t, 0)))
    scratch = [pltpu.VMEM((tile, LANES), F32), pltpu.VMEM((tile, LANES), BF16)]
    for d in DIL_DILATIONS:
        scratch += [pltpu.VMEM((d, blk + tile // d, LANES), BF16)] * 2
    scratch += [pltpu.VMEM((tile, LANES), F32)] * 6
    return pl.pallas_call(
        _dilated_body,
        grid=(DIL_HEADS, s // tile),
        in_specs=in_specs,
        out_specs=pl.BlockSpec((tile, DIL_HEAD_DIM), lambda h, t: (t, h)),
        out_shape=jax.ShapeDtypeStruct((s, DIL_HEADS * DIL_HEAD_DIM), BF16),
        scratch_shapes=scratch,
        compiler_params=_params("arbitrary", "arbitrary"),
        name="dilated",
    )(*([proj] * 10))


def _merge_body(*refs, final):
    (ga_ref, gb_ref, gc_ref, ta0, ta1, tb0, tb1, tc0, tc1, pa_ref, pb_ref, pc_ref, wo_ref,
     h_ref, gate_ref, fw_ref, o_ref) = refs

    def branch(g_ref, t_refs, p_ref):
        y = jnp.dot(g_ref[...], p_ref[...], preferred_element_type=F32)
        gate = jnp.concatenate([t[cblk] for t in t_refs for cblk in range(t.shape[0])], axis=-1).astype(F32)
        return jax.nn.sigmoid(gate) * y

    merged = (branch(ga_ref, (ta0, ta1), pa_ref) + branch(gb_ref, (tb0, tb1), pb_ref)
              + branch(gc_ref, (tc0, tc1), pc_ref))
    y = jnp.dot(merged.astype(BF16), wo_ref[...], preferred_element_type=F32)
    out = h_ref[...] + gate_ref[0] * y
    if final:
        out = out * lax.rsqrt(jnp.mean(out * out, axis=-1, keepdims=True) + EPS) * fw_ref[...]
    o_ref[...] = out


def _merge(ga, gb, gc, proj, pa, pb, pc, wo, h, mod, layer, final_w, final):
    s, d = h.shape
    w = ga.shape[1]
    tm = 256
    const = dict(pipeline_mode=pl.Buffered(1))
    return pl.pallas_call(
        functools.partial(_merge_body, final=final),
        grid=(s // tm,),
        in_specs=[
            pl.BlockSpec((tm, w), lambda i: (i, 0)),
            pl.BlockSpec((tm, w), lambda i: (i, 0)),
            pl.BlockSpec((tm, w), lambda i: (i, 0)),
            *[pl.BlockSpec((GATE_CB, tm, LANES), functools.partial(lambda i, b: (b, i, 0), b=CB_GA // GATE_CB + b))
              for b in range(3 * d // LANES // GATE_CB)],
            pl.BlockSpec((w, d), lambda i: (0, 0), **const),
            pl.BlockSpec((w, d), lambda i: (0, 0), **const),
            pl.BlockSpec((w, d), lambda i: (0, 0), **const),
            pl.BlockSpec((d, d), lambda i: (0, 0), **const),
            pl.BlockSpec((tm, d), lambda i: (i, 0)),
            pl.BlockSpec((1, 1, d), lambda i: (layer, 0, 2)),
            pl.BlockSpec((1, d), lambda i: (0, 0)),
        ],
        out_specs=pl.BlockSpec((tm, d), lambda i: (i, 0)),
        out_shape=jax.ShapeDtypeStruct((s, d), F32),
        compiler_params=_params("arbitrary"),
        name="merge",
    )(ga, gb, gc, *([proj] * 6), pa, pb, pc, wo, h, mod, final_w.reshape(1, d))


def kernel(x, c, w_in, w_proj_a, w_proj_b, w_proj_c, w_out, w_ada, b_ada, norm_w, lam, da_norm_w, ret_norm_w,
           final_norm_w):
    batch, s, d = x.shape
    assert batch == 1 and d == D_MODEL and s % DIL_TILE == 0 and w_in.shape[2] == IN_WIDTH
    depth = w_in.shape[0]
    mod = _adaln(c, w_ada, b_ada)
    h = x[0]
    for layer in range(depth):
        lam_init = 0.8 - 0.6 * math.exp(-0.3 * layer)
        proj = _inproj(h, norm_w[layer], mod, layer, w_in)
        ga = _diffattn(proj, lam[layer], da_norm_w[layer], lam_init)
        gb = _retention(proj, ret_norm_w[layer])
        gc = _dilated(proj)
        h = _merge(ga, gb, gc, proj,
                   w_proj_a[layer].astype(BF16), w_proj_b[layer].astype(BF16), w_proj_c[layer].astype(BF16),
                   w_out[layer].astype(BF16), h, mod, layer, final_norm_w, final=layer == depth - 1)
    return h[None]
```

```python
import functools
import math

import numpy as np
import jax
import jax.numpy as jnp
from jax import lax
from jax.experimental import pallas as pl
from jax.experimental.pallas import tpu as pltpu

F32 = jnp.float32
BF16 = jnp.bfloat16

LANES = 128
VMEM_LIMIT_BYTES = 56 * 1024 * 1024

EPS = 1e-6
LOG2E = math.log2(math.e)

D_MODEL = 2048
DA_HEADS = 8
DA_QK_DIM = 64
DA_V_DIM = 128
RET_HEADS = 8
RET_QK_DIM = 64
RET_V_DIM = 128
RET_CHUNK = 256
ROT_BASE = 10000.0
DIL_DILATIONS = (1, 4, 16)
DIL_RADIUS = 128
DIL_HEADS = 8
DIL_HEAD_DIM = 128
DIL_TILE = 2048
DIL_UNROLL = 16
MASK_VALUE = -1e30

CB_AQ, CB_AK, CB_AV, CB_AZ = 0, 8, 16, 24
CB_BQ, CB_BK, CB_BV, CB_BZ = 32, 36, 40, 48
CB_CQ, CB_CK, CB_CV, CB_CZ = 56, 80, 104, 128
CB_GA, CB_GB, CB_GC = 136, 152, 168
GATE_CB = 8
IN_WIDTH = 184 * LANES


def _params(*sem, flags=None):
    return pltpu.CompilerParams(dimension_semantics=sem, vmem_limit_bytes=VMEM_LIMIT_BYTES, flags=flags)


def _silu(z):
    return z * jax.nn.sigmoid(z)


def _adaln_body(c_ref, w_ref, b_ref, o_ref):
    cc = c_ref[...]
    o_ref[0] = jnp.sum(_silu(cc) * w_ref[0], axis=0, keepdims=True) + b_ref[0]


def _adaln(c, w_ada, b_ada):
    n_layers, d, e = w_ada.shape
    tn = 768
    return pl.pallas_call(
        _adaln_body,
        grid=(n_layers, e // tn),
        in_specs=[
            pl.BlockSpec((d, 1), lambda l, j: (0, 0)),
            pl.BlockSpec((1, d, tn), lambda l, j: (l, 0, j)),
            pl.BlockSpec((1, 1, tn), lambda l, j: (l, 0, j)),
        ],
        out_specs=pl.BlockSpec((1, 1, tn), lambda l, j: (l, 0, j)),
        out_shape=jax.ShapeDtypeStruct((n_layers, 1, e), F32),
        compiler_params=_params("arbitrary", "arbitrary"),
        name="adaln",
    )(c.reshape(d, 1), w_ada, b_ada.reshape(n_layers, 1, e))


def _inproj_body(x_ref, nw_ref, shift_ref, scale_ref, w_ref, o_ref, u_ref, *, tn, a_scale, c_scale):
    j = pl.program_id(1)

    @pl.when(j == 0)
    def _():
        xf = x_ref[...]
        y = xf * lax.rsqrt(jnp.mean(xf * xf, axis=-1, keepdims=True) + EPS) * nw_ref[...]
        u_ref[...] = (y * (1.0 + scale_ref[0]) + shift_ref[0]).astype(BF16)

    acc = jnp.dot(u_ref[...], w_ref[...].astype(BF16), preferred_element_type=F32)
    cb = j * (tn // LANES)
    qscale = jnp.where(cb < CB_AK, a_scale, jnp.where((cb >= CB_CQ) & (cb < CB_CK), c_scale, 1.0))
    acc = acc * qscale
    for cblk in range(tn // LANES):
        o_ref[cblk] = acc[:, cblk * LANES:(cblk + 1) * LANES].astype(BF16)


def _inproj(x, norm_w, mod, layer, w_in):
    s, d = x.shape
    tm, tn = 2048, 512
    body = functools.partial(
        _inproj_body, tn=tn,
        a_scale=DA_QK_DIM ** -0.5 * LOG2E, c_scale=DIL_HEAD_DIM ** -0.5 * LOG2E)
    return pl.pallas_call(
        body,
        grid=(s // tm, IN_WIDTH // tn),
        in_specs=[
            pl.BlockSpec((tm, d), lambda i, j: (i, 0), pipeline_mode=pl.Buffered(1)),
            pl.BlockSpec((1, d), lambda i, j: (0, 0)),
            pl.BlockSpec((1, 1, d), lambda i, j: (layer, 0, 0)),
            pl.BlockSpec((1, 1, d), lambda i, j: (layer, 0, 1)),
            pl.BlockSpec((None, d, tn), lambda i, j: (layer, 0, j)),
        ],
        out_specs=pl.BlockSpec((tn // LANES, tm, LANES), lambda i, j: (j, i, 0)),
        out_shape=jax.ShapeDtypeStruct((IN_WIDTH // LANES, s, LANES), BF16),
        scratch_shapes=[pltpu.VMEM((tm, d), BF16)],
        compiler_params=_params("arbitrary", "arbitrary"),
        name="inproj",
    )(x, norm_w.reshape(1, d), mod, mod, w_in)


DA_HEADS_PER_STEP = 4
DA_QCHUNK = 256
DA_ONES_ROWS = 16


def _diffattn_body(lam_ref, nw_ref, q_ref, k_ref, v_ref, z_ref, o_ref, qs_ref, vt_ref, m_ref, acc_ref,
                   *, tq, lam_init):
    i = pl.program_id(1)
    nh = q_ref.shape[0]
    dv = DA_V_DIM

    @pl.when(i == 0)
    def _():
        ones = jnp.ones((DA_ONES_ROWS, tq), BF16)
        for hh in range(nh):
            def transpose_v(c, carry, hh=hh):
                start = pl.multiple_of(c * tq, tq)
                vt_ref[hh, c, 0:dv, :] = v_ref[hh, pl.ds(start, tq), :].astype(F32).T.astype(BF16)
                vt_ref[hh, c, dv:dv + DA_ONES_ROWS, :] = ones
                return carry

            lax.fori_loop(0, vt_ref.shape[1], transpose_v, 0)

    lane = lax.broadcasted_iota(jnp.int32, (tq, LANES), 1)
    for hh in range(nh):
        q = q_ref[hh]
        zero = jnp.zeros_like(q)
        qs_ref[hh, 0:tq] = jnp.where(lane < DA_QK_DIM, q, zero)
        qs_ref[hh, tq:2 * tq] = jnp.where(lane >= DA_QK_DIM, q, zero)
    m_ref[...] = jnp.full(m_ref.shape, -jnp.inf, F32)
    acc_ref[...] = jnp.zeros(acc_ref.shape, F32)

    def step(kt, diagonal):
        start = pl.multiple_of(kt * tq, tq)
        chains = [(hh, c * DA_QCHUNK) for hh in range(nh) for c in range(2 * tq // DA_QCHUNK)]
        def visible_keys(c0):
            return min(tq, c0 % tq + DA_QCHUNK) if diagonal else tq

        def qk(hh, c0):
            k = k_ref[hh, pl.ds(start, visible_keys(c0)), :]
            st = lax.dot_general(k, qs_ref[hh, c0:c0 + DA_QCHUNK, :], (((1,), (1,)), ((), ())),
                                 preferred_element_type=F32)
            if diagonal:
                key = lax.broadcasted_iota(jnp.int32, st.shape, 0)
                qpos = lax.broadcasted_iota(jnp.int32, st.shape, 1) + (c0 % tq)
                st = jnp.where(key <= qpos, st, -jnp.inf)
            return st.astype(BF16)

        def softmax(hh, c0, st):
            m_prev = m_ref[hh, :, c0:c0 + DA_QCHUNK]
            m_new = jnp.maximum(m_prev, jnp.max(st, axis=0, keepdims=True).astype(F32))
            m_ref[hh, :, c0:c0 + DA_QCHUNK] = m_new
            return jnp.exp2(m_prev - m_new), jnp.exp2(st - m_new.astype(BF16))

        def pv(hh, c0, alpha, pt):
            vt = vt_ref[hh, kt, :, 0:visible_keys(c0)]
            acc_ref[hh, :, c0:c0 + DA_QCHUNK] = (alpha * acc_ref[hh, :, c0:c0 + DA_QCHUNK]
                                                 + jnp.dot(vt, pt, preferred_element_type=F32))

        scores = [qk(*ch) for ch in chains]
        probs = [softmax(*ch, st) for ch, st in zip(chains, scores)]
        for ch, (alpha, pt) in zip(chains, probs):
            pv(*ch, alpha, pt)

    def off_diagonal(kt, carry):
        step(kt, False)
        return carry

    lax.fori_loop(0, i, off_diagonal, 0)
    step(i, True)

    lam = lam_ref[...]
    lam_full = (jnp.exp(jnp.sum(lam[0:1] * lam[1:2], axis=-1, keepdims=True))
                - jnp.exp(jnp.sum(lam[2:3] * lam[3:4], axis=-1, keepdims=True)) + lam_init)
    for hh in range(nh):
        acc = acc_ref[hh]
        inv = 1.0 / acc[dv:dv + 1, :]
        ot = acc[0:dv, 0:tq] * inv[:, 0:tq] - lam_full * (acc[0:dv, tq:2 * tq] * inv[:, tq:2 * tq])
        o = ot.T
        o = o * lax.rsqrt(jnp.mean(o * o, axis=-1, keepdims=True) + EPS) * nw_ref[...] * (1.0 - lam_init)
        o_ref[:, hh * dv:(hh + 1) * dv] = (o * _silu(z_ref[hh].astype(F32))).astype(BF16)


def _diffattn(proj, lam, norm_w, lam_init):
    s = proj.shape[1]
    tq = 512
    nh = DA_HEADS_PER_STEP
    body = functools.partial(_diffattn_body, tq=tq, lam_init=lam_init)
    return pl.pallas_call(
        body,
        grid=(DA_HEADS // nh, s // tq),
        in_specs=[
            pl.BlockSpec((4, DA_QK_DIM), lambda h, i: (0, 0)),
            pl.BlockSpec((1, DA_V_DIM), lambda h, i: (0, 0)),
            pl.BlockSpec((nh, tq, LANES), lambda h, i: (CB_AQ // nh + h, i, 0)),
            pl.BlockSpec((nh, s, LANES), lambda h, i: (CB_AK // nh + h, 0, 0), pipeline_mode=pl.Buffered(1)),
            pl.BlockSpec((nh, s, LANES), lambda h, i: (CB_AV // nh + h, 0, 0), pipeline_mode=pl.Buffered(1)),
            pl.BlockSpec((nh, tq, LANES), lambda h, i: (CB_AZ // nh + h, i, 0)),
        ],
        out_specs=pl.BlockSpec((tq, nh * DA_V_DIM), lambda h, i: (i, h)),
        out_shape=jax.ShapeDtypeStruct((s, DA_HEADS * DA_V_DIM), BF16),
        scratch_shapes=[
            pltpu.VMEM((nh, 2 * tq, LANES), BF16),
            pltpu.VMEM((nh, s // tq, DA_V_DIM + DA_ONES_ROWS, tq), BF16),
            pltpu.VMEM((nh, 1, 2 * tq), F32),
            pltpu.VMEM((nh, DA_V_DIM + DA_ONES_ROWS, 2 * tq), F32),
        ],
        compiler_params=_params("arbitrary", "arbitrary"),
        name="diffattn",
    )(lam, norm_w.reshape(1, DA_V_DIM), proj, proj, proj, proj)


def _retention_tables(s):
    c = RET_CHUNK
    half = RET_QK_DIM // 2
    theta = 1.0 / (ROT_BASE ** jnp.linspace(0.0, 1.0, half, dtype=F32))
    ang = jnp.arange(s, dtype=F32)[:, None] * theta[None, :]
    cos = jnp.tile(jnp.repeat(jnp.cos(ang), 2, axis=-1), (1, 2))
    sign = jnp.tile(jnp.asarray([-1.0, 1.0], F32), RET_QK_DIM)
    sin = jnp.tile(jnp.repeat(jnp.sin(ang), 2, axis=-1), (1, 2)) * sign[None, :]
    log_g = jnp.log1p(-jnp.exp2(-5.0 - jnp.arange(RET_HEADS, dtype=F32)))
    idx = jnp.arange(c, dtype=F32)
    lane_g = jnp.repeat(log_g.reshape(RET_HEADS // 2, 2), RET_QK_DIM, axis=-1)
    qdec = jnp.exp((idx + 1.0)[None, :, None] * lane_g[:, None, :])
    kdec = jnp.exp((c - 1.0 - idx)[None, :, None] * lane_g[:, None, :]) * RET_QK_DIM ** -0.5
    cdec = jnp.broadcast_to(jnp.exp(c * lane_g)[:, :, None], (RET_HEADS // 2, LANES, RET_V_DIM))
    causal = idx[:, None] >= idx[None, :]
    dmask = jnp.where(causal[None], jnp.exp(-c * log_g)[:, None, None], 0.0)
    return cos, sin, qdec, kdec, cdec, dmask.astype(F32)


def _retention_body(cos_ref, sin_ref, qdec_ref, kdec_ref, cdec_ref, dmask_ref, nw_ref,
                    q_ref, k_ref, v_ref, z_ref, o_ref, r_ref):
    @pl.when(pl.program_id(0) == 0)
    def _():
        r_ref[...] = jnp.zeros(r_ref.shape, F32)

    c = q_ref.shape[1]
    cos = cos_ref[...]
    sin = sin_ref[...]
    lane = lax.broadcasted_iota(jnp.int32, (c, LANES), 1)
    even = (lane & 1) == 0
    head0 = lane < RET_QK_DIM

    def rotate(x):
        nxt = pltpu.roll(x, LANES - 1, 1)
        prv = pltpu.roll(x, 1, 1)
        return x * cos + jnp.where(even, nxt, prv) * sin

    row = lax.broadcasted_iota(jnp.int32, r_ref.shape[1:], 0)
    for pp in range(q_ref.shape[0]):
        qd = (rotate(q_ref[pp].astype(F32)) * qdec_ref[pp]).astype(BF16)
        kd = (rotate(k_ref[pp].astype(F32)) * kdec_ref[pp]).astype(BF16)
        zero = jnp.zeros_like(qd)
        state = r_ref[pp]
        state_b = state.astype(BF16)
        updates = []
        for hh in range(2):
            head = 2 * pp + hh
            qz = jnp.where(head0, qd, zero) if hh == 0 else jnp.where(head0, zero, qd)
            sc = lax.dot_general(qz, kd, (((1,), (1,)), ((), ())), preferred_element_type=F32) * dmask_ref[head]
            v = v_ref[head]
            o = (jnp.dot(sc.astype(BF16), v, preferred_element_type=F32)
                 + jnp.dot(qz, state_b, preferred_element_type=F32))
            o = o * lax.rsqrt(jnp.mean(o * o, axis=-1, keepdims=True) + EPS) * nw_ref[...]
            o_ref[:, head * RET_V_DIM:(head + 1) * RET_V_DIM] = (o * _silu(z_ref[head].astype(F32))).astype(BF16)
            updates.append(lax.dot_general(kd, v, (((0,), (0,)), ((), ())), preferred_element_type=F32))
        r_ref[pp] = cdec_ref[pp] * state + jnp.where(row < RET_QK_DIM, updates[0], updates[1])


def _retention(proj, norm_w):
    s = proj.shape[1]
    c = RET_CHUNK
    pairs = RET_HEADS // 2
    cos, sin, qdec, kdec, cdec, dmask = _retention_tables(s)
    return pl.pallas_call(
        _retention_body,
        grid=(s // c,),
        in_specs=[
            pl.BlockSpec((c, LANES), lambda n: (n, 0)),
            pl.BlockSpec((c, LANES), lambda n: (n, 0)),
            pl.BlockSpec((pairs, c, LANES), lambda n: (0, 0, 0)),
            pl.BlockSpec((pairs, c, LANES), lambda n: (0, 0, 0)),
            pl.BlockSpec((pairs, LANES, RET_V_DIM), lambda n: (0, 0, 0)),
            pl.BlockSpec((RET_HEADS, c, c), lambda n: (0, 0, 0)),
            pl.BlockSpec((1, RET_V_DIM), lambda n: (0, 0)),
            pl.BlockSpec((pairs, c, LANES), lambda n: (CB_BQ // pairs, n, 0)),
            pl.BlockSpec((pairs, c, LANES), lambda n: (CB_BK // pairs, n, 0)),
            pl.BlockSpec((RET_HEADS, c, LANES), lambda n: (CB_BV // RET_HEADS, n, 0)),
            pl.BlockSpec((RET_HEADS, c, LANES), lambda n: (CB_BZ // RET_HEADS, n, 0)),
        ],
        out_specs=pl.BlockSpec((c, RET_HEADS * RET_V_DIM), lambda n: (n, 0)),
        out_shape=jax.ShapeDtypeStruct((s, RET_HEADS * RET_V_DIM), BF16),
        scratch_shapes=[pltpu.VMEM((pairs, LANES, RET_V_DIM), F32)],
        compiler_params=_params("arbitrary"),
        name="retention",
    )(cos, sin, qdec, kdec, cdec, dmask, norm_w.reshape(1, RET_V_DIM), proj, proj, proj, proj)


def _dilated_body(*refs):
    (q0, k0, v0, q1, k1, v1, q2, k2, v2, z_ref, o_ref,
     slab, qp, kp0, vp0, kp1, vp1, kp2, vp2, og0, og1, og2, ls0, ls1, ls2) = refs
    t = pl.program_id(1)
    blk = DIL_RADIUS
    groups = (
        (1, q0, k0, v0, kp0, vp0, og0, ls0),
        (4, q1, k1, v1, kp1, vp1, og1, ls1),
        (16, q2, k2, v2, kp2, vp2, og2, ls2),
    )

    qi = lax.broadcasted_iota(jnp.int32, (blk, 2 * blk), 0)
    kj = lax.broadcasted_iota(jnp.int32, (blk, 2 * blk), 1)
    in_window = (kj >= qi) & (kj <= qi + blk)
    bias = jnp.where(in_window, 0.0, MASK_VALUE).astype(F32)
    bias_first = jnp.where(in_window & (kj >= blk), 0.0, MASK_VALUE).astype(F32)

    for d, q_ref, k_ref, v_ref, kp, vp, og, ls in groups:
        m_t = DIL_TILE // d
        nb = m_t // blk

        @pl.when(t == 0)
        def _():
            kp[:, 0:blk, :] = jnp.zeros((d, blk, LANES), BF16)
            vp[:, 0:blk, 0:LANES] = jnp.zeros((d, blk, LANES), BF16)
            vp[:, :, LANES:2 * LANES] = jnp.ones((d, blk + m_t, LANES), BF16)

        if d == 1:
            qp[...] = q_ref[0]
            kp[0, blk:blk + m_t, :] = k_ref[0]
            vp[0, blk:blk + m_t, 0:LANES] = v_ref[0]
        else:
            slab[...] = q_ref[0].astype(F32)
            for r in range(d):
                qp[r * m_t:(r + 1) * m_t, :] = slab[pl.ds(r, m_t, stride=d), :].astype(BF16)
            slab[...] = k_ref[0].astype(F32)
            for r in range(d):
                kp[r, blk:blk + m_t, :] = slab[pl.ds(r, m_t, stride=d), :].astype(BF16)
            slab[...] = v_ref[0].astype(F32)
            for r in range(d):
                vp[r, blk:blk + m_t, 0:LANES] = slab[pl.ds(r, m_t, stride=d), :].astype(BF16)

        def unit(u, carry, d=d, m_t=m_t, nb=nb, kp=kp, vp=vp, og=og, ls=ls):
            r = u // nb
            b = u % nb
            qu = qp[pl.ds(pl.multiple_of(r * m_t + b * blk, blk), blk), :]
            start = pl.multiple_of(b * blk, blk)
            kw = kp[r, pl.ds(start, 2 * blk), :]
            vw = vp[r, pl.ds(start, 2 * blk), :]
            s = lax.dot_general(qu, kw, (((1,), (1,)), ((), ())), preferred_element_type=F32)
            s = s + jnp.where((t == 0) & (b == 0), bias_first, bias)
            m = jnp.max(s, axis=-1, keepdims=True)
            p = jnp.exp2(s - m)
            pv = jnp.dot(p.astype(BF16), vw, preferred_element_type=F32)
            l = pv[:, LANES:2 * LANES]
            o = pv[:, 0:LANES] / l
            lse = m + jnp.log2(l)
            if d == 1:
                row0 = pl.multiple_of(b * blk, blk)
                og[pl.ds(row0, blk), :] = o
                ls[pl.ds(row0, blk), :] = lse
            else:
                row0 = b * (blk * d) + r
                og[pl.ds(row0, blk, stride=d), :] = o
                ls[pl.ds(row0, blk, stride=d), :] = lse
            return carry

        lax.fori_loop(0, d * nb, unit, 0, unroll=DIL_UNROLL)

        kp[:, 0:blk, :] = kp[:, m_t:m_t + blk, :]
        vp[:, 0:blk, 0:LANES] = vp[:, m_t:m_t + blk, 0:LANES]

    l0, l1, l2 = ls0[...], ls1[...], ls2[...]
    mx = jnp.maximum(jnp.maximum(l0, l1), l2)
    w0, w1, w2 = jnp.exp2(l0 - mx), jnp.exp2(l1 - mx), jnp.exp2(l2 - mx)
    o = (w0 * og0[...] + w1 * og1[...] + w2 * og2[...]) / (w0 + w1 + w2)
    o_ref[...] = (o * _silu(z_ref[0].astype(F32))).astype(BF16)


def _dilated(proj):
    s = proj.shape[1]
    tile = DIL_TILE
    blk = DIL_RADIUS

    def qkv_spec(base, g):
        return pl.BlockSpec((1, tile, LANES), lambda h, t: (base + g * DIL_HEADS + h, t, 0))

    in_specs = []
    for g in range(len(DIL_DILATIONS)):
        in_specs += [qkv_spec(CB_CQ, g), qkv_spec(CB_CK, g), qkv_spec(CB_CV, g)]
    in_specs.append(pl.BlockSpec((1, tile, LANES), lambda h, t: (CB_CZ + h, t, 0)))
    scratch = [pltpu.VMEM((tile, LANES), F32), pltpu.VMEM((tile, LANES), BF16)]
    for d in DIL_DILATIONS:
        scratch += [pltpu.VMEM((d, blk + tile // d, LANES), BF16), pltpu.VMEM((d, blk + tile // d, 2 * LANES), BF16)]
    scratch += [pltpu.VMEM((tile, LANES), F32)] * 6
    return pl.pallas_call(
        _dilated_body,
        grid=(DIL_HEADS, s // tile),
        in_specs=in_specs,
        out_specs=pl.BlockSpec((tile, DIL_HEAD_DIM), lambda h, t: (t, h)),
        out_shape=jax.ShapeDtypeStruct((s, DIL_HEADS * DIL_HEAD_DIM), BF16),
        scratch_shapes=scratch,
        compiler_params=_params("arbitrary", "arbitrary"),
        name="dilated",
    )(*([proj] * 10))


def _merge_body(*refs, final):
    (ga_ref, gb_ref, gc_ref, ta0, ta1, tb0, tb1, tc0, tc1, pa_ref, pb_ref, pc_ref, wo_ref,
     h_ref, gate_ref, fw_ref, o_ref) = refs

    def branch(g_ref, t_refs, p_ref):
        y = jnp.dot(g_ref[...], p_ref[...], preferred_element_type=F32)
        gate = jnp.concatenate([t[cblk] for t in t_refs for cblk in range(t.shape[0])], axis=-1).astype(F32)
        return jax.nn.sigmoid(gate) * y

    merged = (branch(ga_ref, (ta0, ta1), pa_ref) + branch(gb_ref, (tb0, tb1), pb_ref)
              + branch(gc_ref, (tc0, tc1), pc_ref))
    y = jnp.dot(merged.astype(BF16), wo_ref[...], preferred_element_type=F32)
    out = h_ref[...] + gate_ref[0] * y
    if final:
        out = out * lax.rsqrt(jnp.mean(out * out, axis=-1, keepdims=True) + EPS) * fw_ref[...]
    o_ref[...] = out


def _merge(ga, gb, gc, proj, pa, pb, pc, wo, h, mod, layer, final_w, final):
    s, d = h.shape
    w = ga.shape[1]
    tm = 256
    const = dict(pipeline_mode=pl.Buffered(1))
    return pl.pallas_call(
        functools.partial(_merge_body, final=final),
        grid=(s // tm,),
        in_specs=[
            pl.BlockSpec((tm, w), lambda i: (i, 0)),
            pl.BlockSpec((tm, w), lambda i: (i, 0)),
            pl.BlockSpec((tm, w), lambda i: (i, 0)),
            *[pl.BlockSpec((GATE_CB, tm, LANES), functools.partial(lambda i, b: (b, i, 0), b=CB_GA // GATE_CB + b))
              for b in range(3 * d // LANES // GATE_CB)],
            pl.BlockSpec((w, d), lambda i: (0, 0), **const),
            pl.BlockSpec((w, d), lambda i: (0, 0), **const),
            pl.BlockSpec((w, d), lambda i: (0, 0), **const),
            pl.BlockSpec((d, d), lambda i: (0, 0), **const),
            pl.BlockSpec((tm, d), lambda i: (i, 0)),
            pl.BlockSpec((1, 1, d), lambda i: (layer, 0, 2)),
            pl.BlockSpec((1, d), lambda i: (0, 0)),
        ],
        out_specs=pl.BlockSpec((tm, d), lambda i: (i, 0)),
        out_shape=jax.ShapeDtypeStruct((s, d), F32),
        compiler_params=_params("arbitrary"),
        name="merge",
    )(ga, gb, gc, *([proj] * 6), pa, pb, pc, wo, h, mod, final_w.reshape(1, d))


def kernel(x, c, w_in, w_proj_a, w_proj_b, w_proj_c, w_out, w_ada, b_ada, norm_w, lam, da_norm_w, ret_norm_w,
           final_norm_w):
    batch, s, d = x.shape
    assert batch == 1 and d == D_MODEL and s % DIL_TILE == 0 and w_in.shape[2] == IN_WIDTH
    depth = w_in.shape[0]
    mod = _adaln(c, w_ada, b_ada)
    h = x[0]
    for layer in range(depth):
        lam_init = 0.8 - 0.6 * math.exp(-0.3 * layer)
        proj = _inproj(h, norm_w[layer], mod, layer, w_in)
        ga = _diffattn(proj, lam[layer], da_norm_w[layer], lam_init)
        gb = _retention(proj, ret_norm_w[layer])
        gc = _dilated(proj)
        h = _merge(ga, gb, gc, proj,
                   w_proj_a[layer].astype(BF16), w_proj_b[layer].astype(BF16), w_proj_c[layer].astype(BF16),
                   w_out[layer].astype(BF16), h, mod, layer, final_norm_w, final=layer == depth - 1)
    return h[None]
```

```python
import functools
import math

import numpy as np
import jax
import jax.numpy as jnp
from jax import lax
from jax.experimental import pallas as pl
from jax.experimental.pallas import tpu as pltpu

F32 = jnp.float32
BF16 = jnp.bfloat16

LANES = 128
VMEM_LIMIT_BYTES = 56 * 1024 * 1024

EPS = 1e-6
LOG2E = math.log2(math.e)

D_MODEL = 2048
DA_HEADS = 8
DA_QK_DIM = 64
DA_V_DIM = 128
RET_HEADS = 8
RET_QK_DIM = 64
RET_V_DIM = 128
RET_CHUNK = 256
ROT_BASE = 10000.0
DIL_DILATIONS = (1, 4, 16)
DIL_RADIUS = 128
DIL_HEADS = 8
DIL_HEAD_DIM = 128
DIL_TILE = 2048
DIL_UNROLL = 16
MASK_VALUE = -1e30

CB_AQ, CB_AK, CB_AV, CB_AZ = 0, 8, 16, 24
CB_BQ, CB_BK, CB_BV, CB_BZ = 32, 36, 40, 48
CB_CQ, CB_CK, CB_CV, CB_CZ = 56, 80, 104, 128
CB_GA, CB_GB, CB_GC = 136, 152, 168
GATE_CB = 8
IN_WIDTH = 184 * LANES


def _params(*sem, flags=None):
    return pltpu.CompilerParams(dimension_semantics=sem, vmem_limit_bytes=VMEM_LIMIT_BYTES, flags=flags)


def _silu(z):
    return z * jax.nn.sigmoid(z)


def _adaln_body(c_ref, w_ref, b_ref, o_ref):
    cc = c_ref[...]
    o_ref[0] = jnp.sum(_silu(cc) * w_ref[0], axis=0, keepdims=True) + b_ref[0]


def _adaln(c, w_ada, b_ada):
    n_layers, d, e = w_ada.shape
    tn = 768
    return pl.pallas_call(
        _adaln_body,
        grid=(n_layers, e // tn),
        in_specs=[
            pl.BlockSpec((d, 1), lambda l, j: (0, 0)),
            pl.BlockSpec((1, d, tn), lambda l, j: (l, 0, j)),
            pl.BlockSpec((1, 1, tn), lambda l, j: (l, 0, j)),
        ],
        out_specs=pl.BlockSpec((1, 1, tn), lambda l, j: (l, 0, j)),
        out_shape=jax.ShapeDtypeStruct((n_layers, 1, e), F32),
        compiler_params=_params("arbitrary", "arbitrary"),
        name="adaln",
    )(c.reshape(d, 1), w_ada, b_ada.reshape(n_layers, 1, e))


def _modulated_norm(xf, nw, scale, shift):
    y = xf * lax.rsqrt(jnp.mean(xf * xf, axis=-1, keepdims=True) + EPS) * nw
    return (y * (1.0 + scale) + shift).astype(BF16)


def _modnorm_body(x_ref, nw_ref, shift_ref, scale_ref, u_ref):
    u_ref[...] = _modulated_norm(x_ref[...], nw_ref[...], scale_ref[0], shift_ref[0])


def _modnorm(x, norm_w, mod, layer):
    s, d = x.shape
    tm = 512
    return pl.pallas_call(
        _modnorm_body,
        grid=(s // tm,),
        in_specs=[
            pl.BlockSpec((tm, d), lambda i: (i, 0)),
            pl.BlockSpec((1, d), lambda i: (0, 0)),
            pl.BlockSpec((1, 1, d), lambda i: (layer, 0, 0)),
            pl.BlockSpec((1, 1, d), lambda i: (layer, 0, 1)),
        ],
        out_specs=pl.BlockSpec((tm, d), lambda i: (i, 0)),
        out_shape=jax.ShapeDtypeStruct((s, d), BF16),
        compiler_params=_params("arbitrary"),
        name="modnorm",
    )(x, norm_w.reshape(1, d), mod, mod)


def _inproj_body(u_ref, w_ref, o_ref, *, tn, a_scale, c_scale):
    j = pl.program_id(1)
    acc = jnp.dot(u_ref[...], w_ref[...].astype(BF16), preferred_element_type=F32)
    cb = j * (tn // LANES)
    qscale = jnp.where(cb < CB_AK, a_scale, jnp.where((cb >= CB_CQ) & (cb < CB_CK), c_scale, 1.0))
    acc = acc * qscale
    for cblk in range(tn // LANES):
        o_ref[cblk] = acc[:, cblk * LANES:(cblk + 1) * LANES].astype(BF16)


def _inproj(u, layer, w_in):
    s, d = u.shape
    tm, tn = 2048, 1024
    body = functools.partial(
        _inproj_body, tn=tn,
        a_scale=DA_QK_DIM ** -0.5 * LOG2E, c_scale=DIL_HEAD_DIM ** -0.5 * LOG2E)
    return pl.pallas_call(
        body,
        grid=(s // tm, IN_WIDTH // tn),
        in_specs=[
            pl.BlockSpec((tm, d), lambda i, j: (i, 0)),
            pl.BlockSpec((None, d, tn), lambda i, j: (layer, 0, j)),
        ],
        out_specs=pl.BlockSpec((tn // LANES, tm, LANES), lambda i, j: (j, i, 0)),
        out_shape=jax.ShapeDtypeStruct((IN_WIDTH // LANES, s, LANES), BF16),
        compiler_params=_params("arbitrary", "arbitrary"),
        name="inproj",
    )(u, w_in)


DA_HEADS_PER_STEP = 4
DA_QCHUNK = 256
DA_ONES_ROWS = 16


def _diffattn_body(lam_ref, nw_ref, q_ref, k_ref, v_ref, z_ref, o_ref, qs_ref, vt_ref, m_ref, acc_ref,
                   *, tq, lam_init):
    i = pl.program_id(1)
    nh = q_ref.shape[0]
    dv = DA_V_DIM

    @pl.when(i == 0)
    def _():
        ones = jnp.ones((DA_ONES_ROWS, tq), BF16)
        for hh in range(nh):
            def transpose_v(c, carry, hh=hh):
                start = pl.multiple_of(c * tq, tq)
                vt_ref[hh, c, 0:dv, :] = v_ref[hh, pl.ds(start, tq), :].astype(F32).T.astype(BF16)
                vt_ref[hh, c, dv:dv + DA_ONES_ROWS, :] = ones
                return carry

            lax.fori_loop(0, vt_ref.shape[1], transpose_v, 0)

    lane = lax.broadcasted_iota(jnp.int32, (tq, LANES), 1)
    for hh in range(nh):
        q = q_ref[hh]
        zero = jnp.zeros_like(q)
        qs_ref[hh, 0:tq] = jnp.where(lane < DA_QK_DIM, q, zero)
        qs_ref[hh, tq:2 * tq] = jnp.where(lane >= DA_QK_DIM, q, zero)
    m_ref[...] = jnp.full(m_ref.shape, -jnp.inf, F32)
    acc_ref[...] = jnp.zeros(acc_ref.shape, F32)

    def step(kt, diagonal):
        start = pl.multiple_of(kt * tq, tq)
        chains = [(hh, c * DA_QCHUNK) for hh in range(nh) for c in range(2 * tq // DA_QCHUNK)]
        def visible_keys(c0):
            return min(tq, c0 % tq + DA_QCHUNK) if diagonal else tq

        def qk(hh, c0):
            k = k_ref[hh, pl.ds(start, visible_keys(c0)), :]
            st = lax.dot_general(k, qs_ref[hh, c0:c0 + DA_QCHUNK, :], (((1,), (1,)), ((), ())),
                                 preferred_element_type=F32)
            if diagonal:
                key = lax.broadcasted_iota(jnp.int32, st.shape, 0)
                qpos = lax.broadcasted_iota(jnp.int32, st.shape, 1) + (c0 % tq)
                st = jnp.where(key <= qpos, st, -jnp.inf)
            return st.astype(BF16)

        def softmax(hh, c0, st):
            m_prev = m_ref[hh, :, c0:c0 + DA_QCHUNK]
            m_new = jnp.maximum(m_prev, jnp.max(st, axis=0, keepdims=True).astype(F32))
            m_ref[hh, :, c0:c0 + DA_QCHUNK] = m_new
            return jnp.exp2(m_prev - m_new), jnp.exp2(st - m_new.astype(BF16))

        def pv(hh, c0, alpha, pt):
            vt = vt_ref[hh, kt, :, 0:visible_keys(c0)]
            acc_ref[hh, :, c0:c0 + DA_QCHUNK] = (alpha * acc_ref[hh, :, c0:c0 + DA_QCHUNK]
                                                 + jnp.dot(vt, pt, preferred_element_type=F32))

        scores = [qk(*ch) for ch in chains]
        probs = [softmax(*ch, st) for ch, st in zip(chains, scores)]
        for ch, (alpha, pt) in zip(chains, probs):
            pv(*ch, alpha, pt)

    def off_diagonal(kt, carry):
        step(kt, False)
        return carry

    lax.fori_loop(0, i, off_diagonal, 0)
    step(i, True)

    lam = lam_ref[...]
    lam_full = (jnp.exp(jnp.sum(lam[0:1] * lam[1:2], axis=-1, keepdims=True))
                - jnp.exp(jnp.sum(lam[2:3] * lam[3:4], axis=-1, keepdims=True)) + lam_init)
    for hh in range(nh):
        acc = acc_ref[hh]
        inv = 1.0 / acc[dv:dv + 1, :]
        ot = acc[0:dv, 0:tq] * inv[:, 0:tq] - lam_full * (acc[0:dv, tq:2 * tq] * inv[:, tq:2 * tq])
        o = ot.T
        o = o * lax.rsqrt(jnp.mean(o * o, axis=-1, keepdims=True) + EPS) * nw_ref[...] * (1.0 - lam_init)
        o_ref[:, hh * dv:(hh + 1) * dv] = (o * _silu(z_ref[hh].astype(F32))).astype(BF16)


def _diffattn(proj, lam, norm_w, lam_init):
    s = proj.shape[1]
    tq = 512
    nh = DA_HEADS_PER_STEP
    body = functools.partial(_diffattn_body, tq=tq, lam_init=lam_init)
    return pl.pallas_call(
        body,
        grid=(DA_HEADS // nh, s // tq),
        in_specs=[
            pl.BlockSpec((4, DA_QK_DIM), lambda h, i: (0, 0)),
            pl.BlockSpec((1, DA_V_DIM), lambda h, i: (0, 0)),
            pl.BlockSpec((nh, tq, LANES), lambda h, i: (CB_AQ // nh + h, i, 0)),
            pl.BlockSpec((nh, s, LANES), lambda h, i: (CB_AK // nh + h, 0, 0), pipeline_mode=pl.Buffered(1)),
            pl.BlockSpec((nh, s, LANES), lambda h, i: (CB_AV // nh + h, 0, 0), pipeline_mode=pl.Buffered(1)),
            pl.BlockSpec((nh, tq, LANES), lambda h, i: (CB_AZ // nh + h, i, 0)),
        ],
        out_specs=pl.BlockSpec((tq, nh * DA_V_DIM), lambda h, i: (i, h)),
        out_shape=jax.ShapeDtypeStruct((s, DA_HEADS * DA_V_DIM), BF16),
        scratch_shapes=[
            pltpu.VMEM((nh, 2 * tq, LANES), BF16),
            pltpu.VMEM((nh, s // tq, DA_V_DIM + DA_ONES_ROWS, tq), BF16),
            pltpu.VMEM((nh, 1, 2 * tq), F32),
            pltpu.VMEM((nh, DA_V_DIM + DA_ONES_ROWS, 2 * tq), F32),
        ],
        compiler_params=_params("arbitrary", "arbitrary"),
        name="diffattn",
    )(lam, norm_w.reshape(1, DA_V_DIM), proj, proj, proj, proj)


def _retention_tables(s):
    c = RET_CHUNK
    half = RET_QK_DIM // 2
    theta = 1.0 / (ROT_BASE ** jnp.linspace(0.0, 1.0, half, dtype=F32))
    ang = jnp.arange(s, dtype=F32)[:, None] * theta[None, :]
    cos = jnp.tile(jnp.repeat(jnp.cos(ang), 2, axis=-1), (1, 2))
    sign = jnp.tile(jnp.asarray([-1.0, 1.0], F32), RET_QK_DIM)
    sin = jnp.tile(jnp.repeat(jnp.sin(ang), 2, axis=-1), (1, 2)) * sign[None, :]
    log_g = jnp.log1p(-jnp.exp2(-5.0 - jnp.arange(RET_HEADS, dtype=F32)))
    idx = jnp.arange(c, dtype=F32)
    lane_g = jnp.repeat(log_g.reshape(RET_HEADS // 2, 2), RET_QK_DIM, axis=-1)
    qdec = jnp.exp((idx + 1.0)[None, :, None] * lane_g[:, None, :])
    kdec = jnp.exp((c - 1.0 - idx)[None, :, None] * lane_g[:, None, :]) * RET_QK_DIM ** -0.5
    cdec = jnp.broadcast_to(jnp.exp(c * lane_g)[:, :, None], (RET_HEADS // 2, LANES, RET_V_DIM))
    causal = idx[:, None] >= idx[None, :]
    dmask = jnp.where(causal[None], jnp.exp(-c * log_g)[:, None, None], 0.0)
    return cos, sin, qdec, kdec, cdec, dmask.astype(F32)


def _retention_body(cos_ref, sin_ref, qdec_ref, kdec_ref, cdec_ref, dmask_ref, nw_ref,
                    q_ref, k_ref, v_ref, z_ref, o_ref, r_ref):
    @pl.when(pl.program_id(0) == 0)
    def _():
        r_ref[...] = jnp.zeros(r_ref.shape, F32)

    c = q_ref.shape[1]
    cos = cos_ref[...]
    sin = sin_ref[...]
    lane = lax.broadcasted_iota(jnp.int32, (c, LANES), 1)
    even = (lane & 1) == 0
    head0 = lane < RET_QK_DIM

    def rotate(x):
        nxt = pltpu.roll(x, LANES - 1, 1)
        prv = pltpu.roll(x, 1, 1)
        return x * cos + jnp.where(even, nxt, prv) * sin

    row = lax.broadcasted_iota(jnp.int32, r_ref.shape[1:], 0)
    for pp in range(q_ref.shape[0]):
        qd = (rotate(q_ref[pp].astype(F32)) * qdec_ref[pp]).astype(BF16)
        kd = (rotate(k_ref[pp].astype(F32)) * kdec_ref[pp]).astype(BF16)
        zero = jnp.zeros_like(qd)
        state = r_ref[pp]
        state_b = state.astype(BF16)
        updates = []
        for hh in range(2):
            head = 2 * pp + hh
            qz = jnp.where(head0, qd, zero) if hh == 0 else jnp.where(head0, zero, qd)
            sc = lax.dot_general(qz, kd, (((1,), (1,)), ((), ())), preferred_element_type=F32) * dmask_ref[head]
            v = v_ref[head]
            o = (jnp.dot(sc.astype(BF16), v, preferred_element_type=F32)
                 + jnp.dot(qz, state_b, preferred_element_type=F32))
            o = o * lax.rsqrt(jnp.mean(o * o, axis=-1, keepdims=True) + EPS) * nw_ref[...]
            o_ref[:, head * RET_V_DIM:(head + 1) * RET_V_DIM] = (o * _silu(z_ref[head].astype(F32))).astype(BF16)
            updates.append(lax.dot_general(kd, v, (((0,), (0,)), ((), ())), preferred_element_type=F32))
        r_ref[pp] = cdec_ref[pp] * state + jnp.where(row < RET_QK_DIM, updates[0], updates[1])


def _retention(proj, norm_w):
    s = proj.shape[1]
    c = RET_CHUNK
    pairs = RET_HEADS // 2
    cos, sin, qdec, kdec, cdec, dmask = _retention_tables(s)
    return pl.pallas_call(
        _retention_body,
        grid=(s // c,),
        in_specs=[
            pl.BlockSpec((c, LANES), lambda n: (n, 0)),
            pl.BlockSpec((c, LANES), lambda n: (n, 0)),
            pl.BlockSpec((pairs, c, LANES), lambda n: (0, 0, 0)),
            pl.BlockSpec((pairs, c, LANES), lambda n: (0, 0, 0)),
            pl.BlockSpec((pairs, LANES, RET_V_DIM), lambda n: (0, 0, 0)),
            pl.BlockSpec((RET_HEADS, c, c), lambda n: (0, 0, 0)),
            pl.BlockSpec((1, RET_V_DIM), lambda n: (0, 0)),
            pl.BlockSpec((pairs, c, LANES), lambda n: (CB_BQ // pairs, n, 0)),
            pl.BlockSpec((pairs, c, LANES), lambda n: (CB_BK // pairs, n, 0)),
            pl.BlockSpec((RET_HEADS, c, LANES), lambda n: (CB_BV // RET_HEADS, n, 0)),
            pl.BlockSpec((RET_HEADS, c, LANES), lambda n: (CB_BZ // RET_HEADS, n, 0)),
        ],
        out_specs=pl.BlockSpec((c, RET_HEADS * RET_V_DIM), lambda n: (n, 0)),
        out_shape=jax.ShapeDtypeStruct((s, RET_HEADS * RET_V_DIM), BF16),
        scratch_shapes=[pltpu.VMEM((pairs, LANES, RET_V_DIM), F32)],
        compiler_params=_params("arbitrary"),
        name="retention",
    )(cos, sin, qdec, kdec, cdec, dmask, norm_w.reshape(1, RET_V_DIM), proj, proj, proj, proj)


def _dilated_body(*refs):
    (q0, k0, v0, q1, k1, v1, q2, k2, v2, z_ref, o_ref,
     slab, qp, kp0, vp0, kp1, vp1, kp2, vp2, og0, og1, og2, ls0, ls1, ls2) = refs
    t = pl.program_id(1)
    blk = DIL_RADIUS
    groups = (
        (1, q0, k0, v0, kp0, vp0, og0, ls0),
        (4, q1, k1, v1, kp1, vp1, og1, ls1),
        (16, q2, k2, v2, kp2, vp2, og2, ls2),
    )

    qi = lax.broadcasted_iota(jnp.int32, (blk, 2 * blk), 0)
    kj = lax.broadcasted_iota(jnp.int32, (blk, 2 * blk), 1)
    in_window = (kj >= qi) & (kj <= qi + blk)
    bias = jnp.where(in_window, 0.0, MASK_VALUE).astype(F32)
    bias_first = jnp.where(in_window & (kj >= blk), 0.0, MASK_VALUE).astype(F32)

    for d, q_ref, k_ref, v_ref, kp, vp, og, ls in groups:
        m_t = DIL_TILE // d
        nb = m_t // blk

        @pl.when(t == 0)
        def _():
            kp[:, 0:blk, :] = jnp.zeros((d, blk, LANES), BF16)
            vp[:, 0:blk, 0:LANES] = jnp.zeros((d, blk, LANES), BF16)
            vp[:, :, LANES:2 * LANES] = jnp.ones((d, blk + m_t, LANES), BF16)

        if d == 1:
            qp[...] = q_ref[0]
            kp[0, blk:blk + m_t, :] = k_ref[0]
            vp[0, blk:blk + m_t, 0:LANES] = v_ref[0]
        else:
            slab[...] = q_ref[0].astype(F32)
            for r in range(d):
                qp[r * m_t:(r + 1) * m_t, :] = slab[pl.ds(r, m_t, stride=d), :].astype(BF16)
            slab[...] = k_ref[0].astype(F32)
            for r in range(d):
                kp[r, blk:blk + m_t, :] = slab[pl.ds(r, m_t, stride=d), :].astype(BF16)
            slab[...] = v_ref[0].astype(F32)
            for r in range(d):
                vp[r, blk:blk + m_t, 0:LANES] = slab[pl.ds(r, m_t, stride=d), :].astype(BF16)

        def unit(u, carry, d=d, m_t=m_t, nb=nb, kp=kp, vp=vp, og=og, ls=ls):
            r = u // nb
            b = u % nb
            qu = qp[pl.ds(pl.multiple_of(r * m_t + b * blk, blk), blk), :]
            start = pl.multiple_of(b * blk, blk)
            kw = kp[r, pl.ds(start, 2 * blk), :]
            vw = vp[r, pl.ds(start, 2 * blk), :]
            s = lax.dot_general(qu, kw, (((1,), (1,)), ((), ())), preferred_element_type=F32)
            s = s + jnp.where((t == 0) & (b == 0), bias_first, bias)
            m = jnp.max(s, axis=-1, keepdims=True)
            p = jnp.exp2(s - m)
            pv = jnp.dot(p.astype(BF16), vw, preferred_element_type=F32)
            l = pv[:, LANES:2 * LANES]
            o = pv[:, 0:LANES] / l
            lse = m + jnp.log2(l)
            if d == 1:
                row0 = pl.multiple_of(b * blk, blk)
                og[pl.ds(row0, blk), :] = o
                ls[pl.ds(row0, blk), :] = lse
            else:
                row0 = b * (blk * d) + r
                og[pl.ds(row0, blk, stride=d), :] = o
                ls[pl.ds(row0, blk, stride=d), :] = lse
            return carry

        lax.fori_loop(0, d * nb, unit, 0, unroll=DIL_UNROLL)

        kp[:, 0:blk, :] = kp[:, m_t:m_t + blk, :]
        vp[:, 0:blk, 0:LANES] = vp[:, m_t:m_t + blk, 0:LANES]

    l0, l1, l2 = ls0[...], ls1[...], ls2[...]
    mx = jnp.maximum(jnp.maximum(l0, l1), l2)
    w0, w1, w2 = jnp.exp2(l0 - mx), jnp.exp2(l1 - mx), jnp.exp2(l2 - mx)
    o = (w0 * og0[...] + w1 * og1[...] + w2 * og2[...]) / (w0 + w1 + w2)
    o_ref[...] = (o * _silu(z_ref[0].astype(F32))).astype(BF16)


def _dilated(proj):
    s = proj.shape[1]
    tile = DIL_TILE
    blk = DIL_RADIUS

    def qkv_spec(base, g):
        return pl.BlockSpec((1, tile, LANES), lambda h, t: (base + g * DIL_HEADS + h, t, 0))

    in_specs = []
    for g in range(len(DIL_DILATIONS)):
        in_specs += [qkv_spec(CB_CQ, g), qkv_spec(CB_CK, g), qkv_spec(CB_CV, g)]
    in_specs.append(pl.BlockSpec((1, tile, LANES), lambda h, t: (CB_CZ + h, t, 0)))
    scratch = [pltpu.VMEM((tile, LANES), F32), pltpu.VMEM((tile, LANES), BF16)]
    for d in DIL_DILATIONS:
        scratch += [pltpu.VMEM((d, blk + tile // d, LANES), BF16), pltpu.VMEM((d, blk + tile // d, 2 * LANES), BF16)]
    scratch += [pltpu.VMEM((tile, LANES), F32)] * 6
    return pl.pallas_call(
        _dilated_body,
        grid=(DIL_HEADS, s // tile),
        in_specs=in_specs,
        out_specs=pl.BlockSpec((tile, DIL_HEAD_DIM), lambda h, t: (t, h)),
        out_shape=jax.ShapeDtypeStruct((s, DIL_HEADS * DIL_HEAD_DIM), BF16),
        scratch_shapes=scratch,
        compiler_params=_params("arbitrary", "arbitrary"),
        name="dilated",
    )(*([proj] * 10))


def _merge_body(*refs, final):
    (ga_ref, gb_ref, gc_ref, ta0, ta1, tb0, tb1, tc0, tc1, pa_ref, pb_ref, pc_ref, wo_ref,
     h_ref, gate_ref, nw_ref) = refs[:16]

    def branch(g_ref, t_refs, p_ref):
        y = jnp.dot(g_ref[...], p_ref[...], preferred_element_type=F32)
        gate = jnp.concatenate([t[cblk] for t in t_refs for cblk in range(t.shape[0])], axis=-1).astype(F32)
        return jax.nn.sigmoid(gate) * y

    merged = (branch(ga_ref, (ta0, ta1), pa_ref) + branch(gb_ref, (tb0, tb1), pb_ref)
              + branch(gc_ref, (tc0, tc1), pc_ref))
    y = jnp.dot(merged.astype(BF16), wo_ref[...], preferred_element_type=F32)
    out = h_ref[...] + gate_ref[0] * y
    if final:
        (o_ref,) = refs[16:]
        o_ref[...] = out * lax.rsqrt(jnp.mean(out * out, axis=-1, keepdims=True) + EPS) * nw_ref[...]
    else:
        shift_ref, scale_ref, o_ref, u_ref = refs[16:]
        o_ref[...] = out
        u_ref[...] = _modulated_norm(out, nw_ref[...], scale_ref[0], shift_ref[0])


def _merge(ga, gb, gc, proj, pa, pb, pc, wo, h, mod, layer, tail_norm_w, final):
    s, d = h.shape
    w = ga.shape[1]
    tm = 256
    const = dict(pipeline_mode=pl.Buffered(1))
    row_spec = pl.BlockSpec((tm, d), lambda i: (i, 0))
    tail_specs, tail_args = [], []
    out_specs, out_shape = row_spec, jax.ShapeDtypeStruct((s, d), F32)
    if not final:
        tail_specs = [pl.BlockSpec((1, 1, d), lambda i: (layer + 1, 0, 0)),
                      pl.BlockSpec((1, 1, d), lambda i: (layer + 1, 0, 1))]
        tail_args = [mod, mod]
        out_specs, out_shape = [row_spec, row_spec], [out_shape, jax.ShapeDtypeStruct((s, d), BF16)]
    return pl.pallas_call(
        functools.partial(_merge_body, final=final),
        grid=(s // tm,),
        out_specs=out_specs,
        out_shape=out_shape,
        in_specs=[
            pl.BlockSpec((tm, w), lambda i: (i, 0)),
            pl.BlockSpec((tm, w), lambda i: (i, 0)),
            pl.BlockSpec((tm, w), lambda i: (i, 0)),
            *[pl.BlockSpec((GATE_CB, tm, LANES), functools.partial(lambda i, b: (b, i, 0), b=CB_GA // GATE_CB + b))
              for b in range(3 * d // LANES // GATE_CB)],
            pl.BlockSpec((w, d), lambda i: (0, 0), **const),
            pl.BlockSpec((w, d), lambda i: (0, 0), **const),
            pl.BlockSpec((w, d), lambda i: (0, 0), **const),
            pl.BlockSpec((d, d), lambda i: (0, 0), **const),
            row_spec,
            pl.BlockSpec((1, 1, d), lambda i: (layer, 0, 2)),
            pl.BlockSpec((1, d), lambda i: (0, 0)),
            *tail_specs,
        ],
        compiler_params=_params("arbitrary"),
        name="merge",
    )(ga, gb, gc, *([proj] * 6), pa, pb, pc, wo, h, mod, tail_norm_w.reshape(1, d), *tail_args)


def kernel(x, c, w_in, w_proj_a, w_proj_b, w_proj_c, w_out, w_ada, b_ada, norm_w, lam, da_norm_w, ret_norm_w,
           final_norm_w):
    batch, s, d = x.shape
    assert batch == 1 and d == D_MODEL and s % DIL_TILE == 0 and w_in.shape[2] == IN_WIDTH
    depth = w_in.shape[0]
    mod = _adaln(c, w_ada, b_ada)
    h = x[0]
    u = _modnorm(h, norm_w[0], mod, 0)
    for layer in range(depth):
        final = layer == depth - 1
        lam_init = 0.8 - 0.6 * math.exp(-0.3 * layer)
        proj = _inproj(u, layer, w_in)
        ga = _diffattn(proj, lam[layer], da_norm_w[layer], lam_init)
        gb = _retention(proj, ret_norm_w[layer])
        gc = _dilated(proj)
        res = _merge(ga, gb, gc, proj,
                     w_proj_a[layer].astype(BF16), w_proj_b[layer].astype(BF16), w_proj_c[layer].astype(BF16),
                     w_out[layer].astype(BF16), h, mod, layer,
                     final_norm_w if final else norm_w[layer + 1], final)
        if final:
            return res[None]
        h, u = res
```

```python
import functools
import math

import numpy as np
import jax
import jax.numpy as jnp
from jax import lax
from jax.experimental import pallas as pl
from jax.experimental.pallas import tpu as pltpu

F32 = jnp.float32
BF16 = jnp.bfloat16

LANES = 128
VMEM_LIMIT_BYTES = 56 * 1024 * 1024

EPS = 1e-6
LOG2E = math.log2(math.e)

D_MODEL = 2048
DA_HEADS = 8
DA_QK_DIM = 64
DA_V_DIM = 128
RET_HEADS = 8
RET_QK_DIM = 64
RET_V_DIM = 128
RET_CHUNK = 256
ROT_BASE = 10000.0
DIL_DILATIONS = (1, 4, 16)
DIL_RADIUS = 128
DIL_HEADS = 8
DIL_HEAD_DIM = 128
DIL_TILE = 2048
DIL_UNROLL = 16
MASK_VALUE = -1e30

CB_AQ, CB_AK, CB_AV, CB_AZ = 0, 8, 16, 24
CB_BQ, CB_BK, CB_BV, CB_BZ = 32, 36, 40, 48
CB_CQ, CB_CK, CB_CV, CB_CZ = 56, 80, 104, 128
CB_GA, CB_GB, CB_GC = 136, 152, 168
GATE_CB = 8
IN_WIDTH = 184 * LANES


def _params(*sem, flags=None):
    return pltpu.CompilerParams(dimension_semantics=sem, vmem_limit_bytes=VMEM_LIMIT_BYTES, flags=flags)


def _silu(z):
    return z * jax.nn.sigmoid(z)


def _adaln_body(c_ref, w_ref, b_ref, o_ref):
    cc = c_ref[...]
    o_ref[0] = jnp.sum(_silu(cc) * w_ref[0], axis=0, keepdims=True) + b_ref[0]


def _adaln(c, w_ada, b_ada):
    n_layers, d, e = w_ada.shape
    tn = 768
    return pl.pallas_call(
        _adaln_body,
        grid=(n_layers, e // tn),
        in_specs=[
            pl.BlockSpec((d, 1), lambda l, j: (0, 0)),
            pl.BlockSpec((1, d, tn), lambda l, j: (l, 0, j)),
            pl.BlockSpec((1, 1, tn), lambda l, j: (l, 0, j)),
        ],
        out_specs=pl.BlockSpec((1, 1, tn), lambda l, j: (l, 0, j)),
        out_shape=jax.ShapeDtypeStruct((n_layers, 1, e), F32),
        compiler_params=_params("arbitrary", "arbitrary"),
        name="adaln",
    )(c.reshape(d, 1), w_ada, b_ada.reshape(n_layers, 1, e))


def _modulated_norm(xf, nw, scale, shift):
    y = xf * lax.rsqrt(jnp.mean(xf * xf, axis=-1, keepdims=True) + EPS) * nw
    return (y * (1.0 + scale) + shift).astype(BF16)


def _modnorm_body(x_ref, nw_ref, shift_ref, scale_ref, u_ref):
    u_ref[...] = _modulated_norm(x_ref[...], nw_ref[...], scale_ref[0], shift_ref[0])


def _modnorm(x, norm_w, mod, layer):
    s, d = x.shape
    tm = 512
    return pl.pallas_call(
        _modnorm_body,
        grid=(s // tm,),
        in_specs=[
            pl.BlockSpec((tm, d), lambda i: (i, 0)),
            pl.BlockSpec((1, d), lambda i: (0, 0)),
            pl.BlockSpec((1, 1, d), lambda i: (layer, 0, 0)),
            pl.BlockSpec((1, 1, d), lambda i: (layer, 0, 1)),
        ],
        out_specs=pl.BlockSpec((tm, d), lambda i: (i, 0)),
        out_shape=jax.ShapeDtypeStruct((s, d), BF16),
        compiler_params=_params("arbitrary"),
        name="modnorm",
    )(x, norm_w.reshape(1, d), mod, mod)


def _inproj_body(u_ref, w_ref, o_ref, *, tn, a_scale, c_scale):
    j = pl.program_id(1)
    acc = jnp.dot(u_ref[...], w_ref[...].astype(BF16), preferred_element_type=F32)
    cb = j * (tn // LANES)
    qscale = jnp.where(cb < CB_AK, a_scale, jnp.where((cb >= CB_CQ) & (cb < CB_CK), c_scale, 1.0))
    acc = acc * qscale
    for cblk in range(tn // LANES):
        o_ref[cblk] = acc[:, cblk * LANES:(cblk + 1) * LANES].astype(BF16)


def _inproj(u, layer, w_in):
    s, d = u.shape
    tm, tn = 2048, 1024
    body = functools.partial(
        _inproj_body, tn=tn,
        a_scale=DA_QK_DIM ** -0.5 * LOG2E, c_scale=DIL_HEAD_DIM ** -0.5 * LOG2E)
    return pl.pallas_call(
        body,
        grid=(s // tm, IN_WIDTH // tn),
        in_specs=[
            pl.BlockSpec((tm, d), lambda i, j: (i, 0)),
            pl.BlockSpec((None, d, tn), lambda i, j: (layer, 0, j)),
        ],
        out_specs=pl.BlockSpec((tn // LANES, tm, LANES), lambda i, j: (j, i, 0)),
        out_shape=jax.ShapeDtypeStruct((IN_WIDTH // LANES, s, LANES), BF16),
        compiler_params=_params("arbitrary", "arbitrary"),
        name="inproj",
    )(u, w_in)


DA_HEADS_PER_STEP = 4
DA_SKEW = 4
DA_QCHUNK = 256
DA_ONES_ROWS = 16


def _diffattn_body(lam_ref, nw_ref, q_ref, k_ref, v_ref, z_ref, o_ref, qs_ref, vt_ref, m_ref, acc_ref,
                   *, tq, lam_init):
    i = pl.program_id(1)
    nh = q_ref.shape[0]
    dv = DA_V_DIM

    @pl.when(i == 0)
    def _():
        ones = jnp.ones((DA_ONES_ROWS, tq), BF16)
        for hh in range(nh):
            def transpose_v(c, carry, hh=hh):
                start = pl.multiple_of(c * tq, tq)
                vt_ref[hh, c, 0:dv, :] = v_ref[hh, pl.ds(start, tq), :].astype(F32).T.astype(BF16)
                vt_ref[hh, c, dv:dv + DA_ONES_ROWS, :] = ones
                return carry

            lax.fori_loop(0, vt_ref.shape[1], transpose_v, 0)

    lane = lax.broadcasted_iota(jnp.int32, (tq, LANES), 1)
    for hh in range(nh):
        q = q_ref[hh]
        zero = jnp.zeros_like(q)
        qs_ref[hh, 0:tq] = jnp.where(lane < DA_QK_DIM, q, zero)
        qs_ref[hh, tq:2 * tq] = jnp.where(lane >= DA_QK_DIM, q, zero)
    m_ref[...] = jnp.full(m_ref.shape, -jnp.inf, F32)
    acc_ref[...] = jnp.zeros(acc_ref.shape, F32)

    def step(kt, diagonal):
        start = pl.multiple_of(kt * tq, tq)
        chains = [(hh, c * DA_QCHUNK) for hh in range(nh) for c in range(2 * tq // DA_QCHUNK)]
        def visible_keys(c0):
            return min(tq, c0 % tq + DA_QCHUNK) if diagonal else tq

        def qk(hh, c0):
            k = k_ref[hh, pl.ds(start, visible_keys(c0)), :]
            st = lax.dot_general(k, qs_ref[hh, c0:c0 + DA_QCHUNK, :], (((1,), (1,)), ((), ())),
                                 preferred_element_type=F32)
            if diagonal:
                key = lax.broadcasted_iota(jnp.int32, st.shape, 0)
                qpos = lax.broadcasted_iota(jnp.int32, st.shape, 1) + (c0 % tq)
                st = jnp.where(key <= qpos, st, -jnp.inf)
            return st.astype(BF16)

        def softmax(hh, c0, st):
            m_prev = m_ref[hh, :, c0:c0 + DA_QCHUNK]
            m_new = jnp.maximum(m_prev, jnp.max(st, axis=0, keepdims=True).astype(F32))
            m_ref[hh, :, c0:c0 + DA_QCHUNK] = m_new
            return jnp.exp2(m_prev - m_new), jnp.exp2(st - m_new.astype(BF16))

        def pv(hh, c0, alpha, pt):
            vt = vt_ref[hh, kt, :, 0:visible_keys(c0)]
            acc_ref[hh, :, c0:c0 + DA_QCHUNK] = (alpha * acc_ref[hh, :, c0:c0 + DA_QCHUNK]
                                                 + jnp.dot(vt, pt, preferred_element_type=F32))

        scores = {}
        for n in range(len(chains) + DA_SKEW):
            if n < len(chains):
                scores[n] = qk(*chains[n])
            if n >= DA_SKEW:
                ch = chains[n - DA_SKEW]
                pv(*ch, *softmax(*ch, scores.pop(n - DA_SKEW)))

    def off_diagonal(kt, carry):
        step(kt, False)
        return carry

    lax.fori_loop(0, i, off_diagonal, 0)
    step(i, True)

    lam = lam_ref[...]
    lam_full = (jnp.exp(jnp.sum(lam[0:1] * lam[1:2], axis=-1, keepdims=True))
                - jnp.exp(jnp.sum(lam[2:3] * lam[3:4], axis=-1, keepdims=True)) + lam_init)
    for hh in range(nh):
        acc = acc_ref[hh]
        inv = 1.0 / acc[dv:dv + 1, :]
        ot = acc[0:dv, 0:tq] * inv[:, 0:tq] - lam_full * (acc[0:dv, tq:2 * tq] * inv[:, tq:2 * tq])
        o = ot.T
        o = o * lax.rsqrt(jnp.mean(o * o, axis=-1, keepdims=True) + EPS) * nw_ref[...] * (1.0 - lam_init)
        o_ref[:, hh * dv:(hh + 1) * dv] = (o * _silu(z_ref[hh].astype(F32))).astype(BF16)


def _diffattn(proj, lam, norm_w, lam_init):
    s = proj.shape[1]
    tq = 512
    nh = DA_HEADS_PER_STEP
    body = functools.partial(_diffattn_body, tq=tq, lam_init=lam_init)
    return pl.pallas_call(
        body,
        grid=(DA_HEADS // nh, s // tq),
        in_specs=[
            pl.BlockSpec((4, DA_QK_DIM), lambda h, i: (0, 0)),
            pl.BlockSpec((1, DA_V_DIM), lambda h, i: (0, 0)),
            pl.BlockSpec((nh, tq, LANES), lambda h, i: (CB_AQ // nh + h, i, 0)),
            pl.BlockSpec((nh, s, LANES), lambda h, i: (CB_AK // nh + h, 0, 0), pipeline_mode=pl.Buffered(1)),
            pl.BlockSpec((nh, s, LANES), lambda h, i: (CB_AV // nh + h, 0, 0), pipeline_mode=pl.Buffered(1)),
            pl.BlockSpec((nh, tq, LANES), lambda h, i: (CB_AZ // nh + h, i, 0)),
        ],
        out_specs=pl.BlockSpec((tq, nh * DA_V_DIM), lambda h, i: (i, h)),
        out_shape=jax.ShapeDtypeStruct((s, DA_HEADS * DA_V_DIM), BF16),
        scratch_shapes=[
            pltpu.VMEM((nh, 2 * tq, LANES), BF16),
            pltpu.VMEM((nh, s // tq, DA_V_DIM + DA_ONES_ROWS, tq), BF16),
            pltpu.VMEM((nh, 1, 2 * tq), F32),
            pltpu.VMEM((nh, DA_V_DIM + DA_ONES_ROWS, 2 * tq), F32),
        ],
        compiler_params=_params("arbitrary", "arbitrary"),
        name="diffattn",
    )(lam, norm_w.reshape(1, DA_V_DIM), proj, proj, proj, proj)


def _retention_tables(s):
    c = RET_CHUNK
    half = RET_QK_DIM // 2
    theta = 1.0 / (ROT_BASE ** jnp.linspace(0.0, 1.0, half, dtype=F32))
    ang = jnp.arange(s, dtype=F32)[:, None] * theta[None, :]
    cos = jnp.tile(jnp.repeat(jnp.cos(ang), 2, axis=-1), (1, 2))
    sign = jnp.tile(jnp.asarray([-1.0, 1.0], F32), RET_QK_DIM)
    sin = jnp.tile(jnp.repeat(jnp.sin(ang), 2, axis=-1), (1, 2)) * sign[None, :]
    log_g = jnp.log1p(-jnp.exp2(-5.0 - jnp.arange(RET_HEADS, dtype=F32)))
    idx = jnp.arange(c, dtype=F32)
    lane_g = jnp.repeat(log_g.reshape(RET_HEADS // 2, 2), RET_QK_DIM, axis=-1)
    qdec = jnp.exp((idx + 1.0)[None, :, None] * lane_g[:, None, :])
    kdec = jnp.exp((c - 1.0 - idx)[None, :, None] * lane_g[:, None, :]) * RET_QK_DIM ** -0.5
    cdec = jnp.broadcast_to(jnp.exp(c * lane_g)[:, :, None], (RET_HEADS // 2, LANES, RET_V_DIM))
    causal = idx[:, None] >= idx[None, :]
    dmask = jnp.where(causal[None], jnp.exp(-c * log_g)[:, None, None], 0.0)
    return cos, sin, qdec, kdec, cdec, dmask.astype(F32)


def _retention_body(cos_ref, sin_ref, qdec_ref, kdec_ref, cdec_ref, dmask_ref, nw_ref,
                    q_ref, k_ref, v_ref, z_ref, o_ref, r_ref):
    @pl.when(pl.program_id(0) == 0)
    def _():
        r_ref[...] = jnp.zeros(r_ref.shape, F32)

    c = q_ref.shape[1]
    cos = cos_ref[...]
    sin = sin_ref[...]
    lane = lax.broadcasted_iota(jnp.int32, (c, LANES), 1)
    even = (lane & 1) == 0
    head0 = lane < RET_QK_DIM

    def rotate(x):
        nxt = pltpu.roll(x, LANES - 1, 1)
        prv = pltpu.roll(x, 1, 1)
        return x * cos + jnp.where(even, nxt, prv) * sin

    row = lax.broadcasted_iota(jnp.int32, r_ref.shape[1:], 0)
    for pp in range(q_ref.shape[0]):
        qd = (rotate(q_ref[pp].astype(F32)) * qdec_ref[pp]).astype(BF16)
        kd = (rotate(k_ref[pp].astype(F32)) * kdec_ref[pp]).astype(BF16)
        zero = jnp.zeros_like(qd)
        state = r_ref[pp]
        state_b = state.astype(BF16)
        updates = []
        for hh in range(2):
            head = 2 * pp + hh
            qz = jnp.where(head0, qd, zero) if hh == 0 else jnp.where(head0, zero, qd)
            sc = lax.dot_general(qz, kd, (((1,), (1,)), ((), ())), preferred_element_type=F32) * dmask_ref[head]
            v = v_ref[head]
            o = (jnp.dot(sc.astype(BF16), v, preferred_element_type=F32)
                 + jnp.dot(qz, state_b, preferred_element_type=F32))
            o = o * lax.rsqrt(jnp.mean(o * o, axis=-1, keepdims=True) + EPS) * nw_ref[...]
            o_ref[:, head * RET_V_DIM:(head + 1) * RET_V_DIM] = (o * _silu(z_ref[head].astype(F32))).astype(BF16)
            updates.append(lax.dot_general(kd, v, (((0,), (0,)), ((), ())), preferred_element_type=F32))
        r_ref[pp] = cdec_ref[pp] * state + jnp.where(row < RET_QK_DIM, updates[0], updates[1])


def _retention(proj, norm_w):
    s = proj.shape[1]
    c = RET_CHUNK
    pairs = RET_HEADS // 2
    cos, sin, qdec, kdec, cdec, dmask = _retention_tables(s)
    return pl.pallas_call(
        _retention_body,
        grid=(s // c,),
        in_specs=[
            pl.BlockSpec((c, LANES), lambda n: (n, 0)),
            pl.BlockSpec((c, LANES), lambda n: (n, 0)),
            pl.BlockSpec((pairs, c, LANES), lambda n: (0, 0, 0)),
            pl.BlockSpec((pairs, c, LANES), lambda n: (0, 0, 0)),
            pl.BlockSpec((pairs, LANES, RET_V_DIM), lambda n: (0, 0, 0)),
            pl.BlockSpec((RET_HEADS, c, c), lambda n: (0, 0, 0)),
            pl.BlockSpec((1, RET_V_DIM), lambda n: (0, 0)),
            pl.BlockSpec((pairs, c, LANES), lambda n: (CB_BQ // pairs, n, 0)),
            pl.BlockSpec((pairs, c, LANES), lambda n: (CB_BK // pairs, n, 0)),
            pl.BlockSpec((RET_HEADS, c, LANES), lambda n: (CB_BV // RET_HEADS, n, 0)),
            pl.BlockSpec((RET_HEADS, c, LANES), lambda n: (CB_BZ // RET_HEADS, n, 0)),
        ],
        out_specs=pl.BlockSpec((c, RET_HEADS * RET_V_DIM), lambda n: (n, 0)),
        out_shape=jax.ShapeDtypeStruct((s, RET_HEADS * RET_V_DIM), BF16),
        scratch_shapes=[pltpu.VMEM((pairs, LANES, RET_V_DIM), F32)],
        compiler_params=_params("arbitrary"),
        name="retention",
    )(cos, sin, qdec, kdec, cdec, dmask, norm_w.reshape(1, RET_V_DIM), proj, proj, proj, proj)


def _dilated_body(*refs):
    (q0, k0, v0, q1, k1, v1, q2, k2, v2, z_ref, o_ref,
     slab, qp, kp0, vp0, kp1, vp1, kp2, vp2, og0, og1, og2, ls0, ls1, ls2) = refs
    t = pl.program_id(1)
    blk = DIL_RADIUS
    groups = (
        (1, q0, k0, v0, kp0, vp0, og0, ls0),
        (4, q1, k1, v1, kp1, vp1, og1, ls1),
        (16, q2, k2, v2, kp2, vp2, og2, ls2),
    )

    qi = lax.broadcasted_iota(jnp.int32, (blk, 2 * blk), 0)
    kj = lax.broadcasted_iota(jnp.int32, (blk, 2 * blk), 1)
    in_window = (kj >= qi) & (kj <= qi + blk)
    bias = jnp.where(in_window, 0.0, MASK_VALUE).astype(F32)
    bias_first = jnp.where(in_window & (kj >= blk), 0.0, MASK_VALUE).astype(F32)

    for d, q_ref, k_ref, v_ref, kp, vp, og, ls in groups:
        m_t = DIL_TILE // d
        nb = m_t // blk

        @pl.when(t == 0)
        def _():
            kp[:, 0:blk, :] = jnp.zeros((d, blk, LANES), BF16)
            vp[:, 0:blk, 0:LANES] = jnp.zeros((d, blk, LANES), BF16)
            vp[:, :, LANES:2 * LANES] = jnp.ones((d, blk + m_t, LANES), BF16)

        if d == 1:
            qp[...] = q_ref[0]
            kp[0, blk:blk + m_t, :] = k_ref[0]
            vp[0, blk:blk + m_t, 0:LANES] = v_ref[0]
        else:
            slab[...] = q_ref[0].astype(F32)
            for r in range(d):
                qp[r * m_t:(r + 1) * m_t, :] = slab[pl.ds(r, m_t, stride=d), :].astype(BF16)
            slab[...] = k_ref[0].astype(F32)
            for r in range(d):
                kp[r, blk:blk + m_t, :] = slab[pl.ds(r, m_t, stride=d), :].astype(BF16)
            slab[...] = v_ref[0].astype(F32)
            for r in range(d):
                vp[r, blk:blk + m_t, 0:LANES] = slab[pl.ds(r, m_t, stride=d), :].astype(BF16)

        def unit(u, carry, d=d, m_t=m_t, nb=nb, kp=kp, vp=vp, og=og, ls=ls):
            r = u // nb
            b = u % nb
            qu = qp[pl.ds(pl.multiple_of(r * m_t + b * blk, blk), blk), :]
            start = pl.multiple_of(b * blk, blk)
            kw = kp[r, pl.ds(start, 2 * blk), :]
            vw = vp[r, pl.ds(start, 2 * blk), :]
            s = lax.dot_general(qu, kw, (((1,), (1,)), ((), ())), preferred_element_type=F32)
            s = s + jnp.where((t == 0) & (b == 0), bias_first, bias)
            m = jnp.max(s, axis=-1, keepdims=True)
            p = jnp.exp2(s - m)
            pv = jnp.dot(p.astype(BF16), vw, preferred_element_type=F32)
            l = pv[:, LANES:2 * LANES]
            o = pv[:, 0:LANES] / l
            lse = m + jnp.log2(l)
            if d == 1:
                row0 = pl.multiple_of(b * blk, blk)
                og[pl.ds(row0, blk), :] = o
                ls[pl.ds(row0, blk), :] = lse
            else:
                row0 = b * (blk * d) + r
                og[pl.ds(row0, blk, stride=d), :] = o
                ls[pl.ds(row0, blk, stride=d), :] = lse
            return carry

        lax.fori_loop(0, d * nb, unit, 0, unroll=DIL_UNROLL)

        kp[:, 0:blk, :] = kp[:, m_t:m_t + blk, :]
        vp[:, 0:blk, 0:LANES] = vp[:, m_t:m_t + blk, 0:LANES]

    l0, l1, l2 = ls0[...], ls1[...], ls2[...]
    mx = jnp.maximum(jnp.maximum(l0, l1), l2)
    w0, w1, w2 = jnp.exp2(l0 - mx), jnp.exp2(l1 - mx), jnp.exp2(l2 - mx)
    o = (w0 * og0[...] + w1 * og1[...] + w2 * og2[...]) / (w0 + w1 + w2)
    o_ref[...] = (o * _silu(z_ref[0].astype(F32))).astype(BF16)


def _dilated(proj):
    s = proj.shape[1]
    tile = DIL_TILE
    blk = DIL_RADIUS

    def qkv_spec(base, g):
        return pl.BlockSpec((1, tile, LANES), lambda h, t: (base + g * DIL_HEADS + h, t, 0))

    in_specs = []
    for g in range(len(DIL_DILATIONS)):
        in_specs += [qkv_spec(CB_CQ, g), qkv_spec(CB_CK, g), qkv_spec(CB_CV, g)]
    in_specs.append(pl.BlockSpec((1, tile, LANES), lambda h, t: (CB_CZ + h, t, 0)))
    scratch = [pltpu.VMEM((tile, LANES), F32), pltpu.VMEM((tile, LANES), BF16)]
    for d in DIL_DILATIONS:
        scratch += [pltpu.VMEM((d, blk + tile // d, LANES), BF16), pltpu.VMEM((d, blk + tile // d, 2 * LANES), BF16)]
    scratch += [pltpu.VMEM((tile, LANES), F32)] * 6
    return pl.pallas_call(
        _dilated_body,
        grid=(DIL_HEADS, s // tile),
        in_specs=in_specs,
        out_specs=pl.BlockSpec((tile, DIL_HEAD_DIM), lambda h, t: (t, h)),
        out_shape=jax.ShapeDtypeStruct((s, DIL_HEADS * DIL_HEAD_DIM), BF16),
        scratch_shapes=scratch,
        compiler_params=_params("arbitrary", "arbitrary"),
        name="dilated",
    )(*([proj] * 10))


def _merge_body(*refs, final):
    (ga_ref, gb_ref, gc_ref, ta0, ta1, tb0, tb1, tc0, tc1, pa_ref, pb_ref, pc_ref, wo_ref,
     h_ref, gate_ref, nw_ref) = refs[:16]

    def branch(g_ref, t_refs, p_ref):
        y = jnp.dot(g_ref[...], p_ref[...], preferred_element_type=F32)
        gate = jnp.concatenate([t[cblk] for t in t_refs for cblk in range(t.shape[0])], axis=-1).astype(F32)
        return jax.nn.sigmoid(gate) * y

    merged = (branch(ga_ref, (ta0, ta1), pa_ref) + branch(gb_ref, (tb0, tb1), pb_ref)
              + branch(gc_ref, (tc0, tc1), pc_ref))
    y = jnp.dot(merged.astype(BF16), wo_ref[...], preferred_element_type=F32)
    out = h_ref[...] + gate_ref[0] * y
    if final:
        (o_ref,) = refs[16:]
        o_ref[...] = out * lax.rsqrt(jnp.mean(out * out, axis=-1, keepdims=True) + EPS) * nw_ref[...]
    else:
        shift_ref, scale_ref, o_ref, u_ref = refs[16:]
        o_ref[...] = out
        u_ref[...] = _modulated_norm(out, nw_ref[...], scale_ref[0], shift_ref[0])


def _merge(ga, gb, gc, proj, pa, pb, pc, wo, h, mod, layer, tail_norm_w, final):
    s, d = h.shape
    w = ga.shape[1]
    tm = 256
    const = dict(pipeline_mode=pl.Buffered(1))
    row_spec = pl.BlockSpec((tm, d), lambda i: (i, 0))
    tail_specs, tail_args = [], []
    out_specs, out_shape = row_spec, jax.ShapeDtypeStruct((s, d), F32)
    if not final:
        tail_specs = [pl.BlockSpec((1, 1, d), lambda i: (layer + 1, 0, 0)),
                      pl.BlockSpec((1, 1, d), lambda i: (layer + 1, 0, 1))]
        tail_args = [mod, mod]
        out_specs, out_shape = [row_spec, row_spec], [out_shape, jax.ShapeDtypeStruct((s, d), BF16)]
    return pl.pallas_call(
        functools.partial(_merge_body, final=final),
        grid=(s // tm,),
        out_specs=out_specs,
        out_shape=out_shape,
        in_specs=[
            pl.BlockSpec((tm, w), lambda i: (i, 0)),
            pl.BlockSpec((tm, w), lambda i: (i, 0)),
            pl.BlockSpec((tm, w), lambda i: (i, 0)),
            *[pl.BlockSpec((GATE_CB, tm, LANES), functools.partial(lambda i, b: (b, i, 0), b=CB_GA // GATE_CB + b))
              for b in range(3 * d // LANES // GATE_CB)],
            pl.BlockSpec((w, d), lambda i: (0, 0), **const),
            pl.BlockSpec((w, d), lambda i: (0, 0), **const),
            pl.BlockSpec((w, d), lambda i: (0, 0), **const),
            pl.BlockSpec((d, d), lambda i: (0, 0), **const),
            row_spec,
            pl.BlockSpec((1, 1, d), lambda i: (layer, 0, 2)),
            pl.BlockSpec((1, d), lambda i: (0, 0)),
            *tail_specs,
        ],
        compiler_params=_params("arbitrary"),
        name="merge",
    )(ga, gb, gc, *([proj] * 6), pa, pb, pc, wo, h, mod, tail_norm_w.reshape(1, d), *tail_args)


def kernel(x, c, w_in, w_proj_a, w_proj_b, w_proj_c, w_out, w_ada, b_ada, norm_w, lam, da_norm_w, ret_norm_w,
           final_norm_w):
    batch, s, d = x.shape
    assert batch == 1 and d == D_MODEL and s % DIL_TILE == 0 and w_in.shape[2] == IN_WIDTH
    depth = w_in.shape[0]
    mod = _adaln(c, w_ada, b_ada)
    h = x[0]
    u = _modnorm(h, norm_w[0], mod, 0)
    for layer in range(depth):
        final = layer == depth - 1
        lam_init = 0.8 - 0.6 * math.exp(-0.3 * layer)
        proj = _inproj(u, layer, w_in)
        ga = _diffattn(proj, lam[layer], da_norm_w[layer], lam_init)
        gb = _retention(proj, ret_norm_w[layer])
        gc = _dilated(proj)
        res = _merge(ga, gb, gc, proj,
                     w_proj_a[layer].astype(BF16), w_proj_b[layer].astype(BF16), w_proj_c[layer].astype(BF16),
                     w_out[layer].astype(BF16), h, mod, layer,
                     final_norm_w if final else norm_w[layer + 1], final)
        if final:
            return res[None]
        h, u = res
```

```python
import functools
import math

import numpy as np
import jax
import jax.numpy as jnp
from jax import lax
from jax.experimental import pallas as pl
from jax.experimental.pallas import tpu as pltpu

F32 = jnp.float32
BF16 = jnp.bfloat16

LANES = 128
VMEM_LIMIT_BYTES = 56 * 1024 * 1024

EPS = 1e-6
LOG2E = math.log2(math.e)

D_MODEL = 2048
DA_HEADS = 8
DA_QK_DIM = 64
DA_V_DIM = 128
RET_HEADS = 8
RET_QK_DIM = 64
RET_V_DIM = 128
RET_CHUNK = 256
ROT_BASE = 10000.0
DIL_DILATIONS = (1, 4, 16)
DIL_RADIUS = 128
DIL_HEADS = 8
DIL_HEAD_DIM = 128
DIL_TILE = 2048
DIL_UNROLL = 16
MASK_VALUE = -1e30

CB_AQ, CB_AK, CB_AV, CB_AZ = 0, 8, 16, 24
CB_BQ, CB_BK, CB_BV, CB_BZ = 32, 36, 40, 48
CB_CQ, CB_CK, CB_CV, CB_CZ = 56, 80, 104, 128
CB_GA, CB_GB, CB_GC = 136, 152, 168
GATE_CB = 8
IN_WIDTH = 184 * LANES


def _params(*sem, flags=None):
    return pltpu.CompilerParams(dimension_semantics=sem, vmem_limit_bytes=VMEM_LIMIT_BYTES, flags=flags)


def _silu(z):
    return z * jax.nn.sigmoid(z)


def _adaln_body(c_ref, w_ref, b_ref, o_ref):
    cc = c_ref[...]
    o_ref[0] = jnp.sum(_silu(cc) * w_ref[0], axis=0, keepdims=True) + b_ref[0]


def _adaln(c, w_ada, b_ada):
    n_layers, d, e = w_ada.shape
    tn = 768
    return pl.pallas_call(
        _adaln_body,
        grid=(n_layers, e // tn),
        in_specs=[
            pl.BlockSpec((d, 1), lambda l, j: (0, 0)),
            pl.BlockSpec((1, d, tn), lambda l, j: (l, 0, j)),
            pl.BlockSpec((1, 1, tn), lambda l, j: (l, 0, j)),
        ],
        out_specs=pl.BlockSpec((1, 1, tn), lambda l, j: (l, 0, j)),
        out_shape=jax.ShapeDtypeStruct((n_layers, 1, e), F32),
        compiler_params=_params("arbitrary", "arbitrary"),
        name="adaln",
    )(c.reshape(d, 1), w_ada, b_ada.reshape(n_layers, 1, e))


def _modulated_norm(xf, nw, scale, shift):
    y = xf * lax.rsqrt(jnp.mean(xf * xf, axis=-1, keepdims=True) + EPS) * nw
    return (y * (1.0 + scale) + shift).astype(BF16)


def _modnorm_body(x_ref, nw_ref, shift_ref, scale_ref, u_ref):
    u_ref[...] = _modulated_norm(x_ref[...], nw_ref[...], scale_ref[0], shift_ref[0])


def _modnorm(x, norm_w, mod, layer):
    s, d = x.shape
    tm = 512
    return pl.pallas_call(
        _modnorm_body,
        grid=(s // tm,),
        in_specs=[
            pl.BlockSpec((tm, d), lambda i: (i, 0)),
            pl.BlockSpec((1, d), lambda i: (0, 0)),
            pl.BlockSpec((1, 1, d), lambda i: (layer, 0, 0)),
            pl.BlockSpec((1, 1, d), lambda i: (layer, 0, 1)),
        ],
        out_specs=pl.BlockSpec((tm, d), lambda i: (i, 0)),
        out_shape=jax.ShapeDtypeStruct((s, d), BF16),
        compiler_params=_params("arbitrary"),
        name="modnorm",
    )(x, norm_w.reshape(1, d), mod, mod)


def _inproj_body(u_ref, w_ref, o_ref, *, tn, a_scale, c_scale):
    j = pl.program_id(1)
    acc = jnp.dot(u_ref[...], w_ref[...].astype(BF16), preferred_element_type=F32)
    cb = j * (tn // LANES)
    qscale = jnp.where(cb < CB_AK, a_scale, jnp.where((cb >= CB_CQ) & (cb < CB_CK), c_scale, 1.0))
    acc = acc * qscale
    for cblk in range(tn // LANES):
        o_ref[cblk] = acc[:, cblk * LANES:(cblk + 1) * LANES].astype(BF16)


def _inproj(u, layer, w_in):
    s, d = u.shape
    tm, tn = 2048, 1024
    body = functools.partial(
        _inproj_body, tn=tn,
        a_scale=DA_QK_DIM ** -0.5 * LOG2E, c_scale=DIL_HEAD_DIM ** -0.5 * LOG2E)
    return pl.pallas_call(
        body,
        grid=(s // tm, IN_WIDTH // tn),
        in_specs=[
            pl.BlockSpec((tm, d), lambda i, j: (i, 0)),
            pl.BlockSpec((None, d, tn), lambda i, j: (layer, 0, j)),
        ],
        out_specs=pl.BlockSpec((tn // LANES, tm, LANES), lambda i, j: (j, i, 0)),
        out_shape=jax.ShapeDtypeStruct((IN_WIDTH // LANES, s, LANES), BF16),
        compiler_params=_params("arbitrary", "arbitrary"),
        name="inproj",
    )(u, w_in)


DA_HEADS_PER_STEP = 4
DA_SKEW = 4
DA_QCHUNK = 256
DA_ONES_ROWS = 16


def _diffattn_body(lam_ref, nw_ref, q_ref, k_ref, v_ref, z_ref, o_ref, qs_ref, vt_ref, m_ref, acc_ref, s_ref,
                   *, tq, lam_init):
    i = pl.program_id(1)
    nh = q_ref.shape[0]
    dv = DA_V_DIM

    @pl.when(i == 0)
    def _():
        ones = jnp.ones((DA_ONES_ROWS, tq), BF16)
        for hh in range(nh):
            def transpose_v(c, carry, hh=hh):
                start = pl.multiple_of(c * tq, tq)
                vt_ref[hh, c, 0:dv, :] = v_ref[hh, pl.ds(start, tq), :].astype(F32).T.astype(BF16)
                vt_ref[hh, c, dv:dv + DA_ONES_ROWS, :] = ones
                return carry

            lax.fori_loop(0, vt_ref.shape[1], transpose_v, 0)

    lane = lax.broadcasted_iota(jnp.int32, (tq, LANES), 1)
    for hh in range(nh):
        q = q_ref[hh]
        zero = jnp.zeros_like(q)
        qs_ref[hh, 0:tq] = jnp.where(lane < DA_QK_DIM, q, zero)
        qs_ref[hh, tq:2 * tq] = jnp.where(lane >= DA_QK_DIM, q, zero)
    m_ref[...] = jnp.full(m_ref.shape, -jnp.inf, F32)
    acc_ref[...] = jnp.zeros(acc_ref.shape, F32)

    chains = [(hh, c * DA_QCHUNK) for hh in range(nh) for c in range(2 * tq // DA_QCHUNK)]

    def visible_keys(c0, diagonal):
        return min(tq, c0 % tq + DA_QCHUNK) if diagonal else tq

    def scores_to_scratch(kt, slot, chain, diagonal):
        hh, c0 = chain
        nk = visible_keys(c0, diagonal)
        k = k_ref[hh, pl.ds(pl.multiple_of(kt * tq, tq), nk), :]
        st = lax.dot_general(k, qs_ref[hh, c0:c0 + DA_QCHUNK, :], (((1,), (1,)), ((), ())),
                             preferred_element_type=F32)
        if diagonal:
            key = lax.broadcasted_iota(jnp.int32, st.shape, 0)
            qpos = lax.broadcasted_iota(jnp.int32, st.shape, 1) + (c0 % tq)
            st = jnp.where(key <= qpos, st, -jnp.inf)
        s_ref[slot, hh, 0:nk, c0:c0 + DA_QCHUNK] = st.astype(BF16)

    def softmax_pv(kt, slot, chain, diagonal):
        hh, c0 = chain
        nk = visible_keys(c0, diagonal)
        st = s_ref[slot, hh, 0:nk, c0:c0 + DA_QCHUNK]
        m_prev = m_ref[hh, :, c0:c0 + DA_QCHUNK]
        m_new = jnp.maximum(m_prev, jnp.max(st, axis=0, keepdims=True).astype(F32))
        m_ref[hh, :, c0:c0 + DA_QCHUNK] = m_new
        pt = jnp.exp2(st - m_new.astype(BF16))
        acc_ref[hh, :, c0:c0 + DA_QCHUNK] = (jnp.exp2(m_prev - m_new) * acc_ref[hh, :, c0:c0 + DA_QCHUNK]
                                             + jnp.dot(vt_ref[hh, kt, :, 0:nk], pt, preferred_element_type=F32))

    def stage(qk=None, sm=None, last=None):
        for n in range(len(chains) + (DA_SKEW if last is not None else 0)):
            if n < len(chains):
                if qk is not None:
                    scores_to_scratch(qk[0], qk[1], chains[n], qk[2])
                if sm is not None:
                    softmax_pv(sm[0], sm[1], chains[n], sm[2])
            if last is not None and n >= DA_SKEW:
                softmax_pv(last[0], last[1], chains[n - DA_SKEW], last[2])

    pairs = jnp.maximum(i - 1, 0) // 2
    odd = (i & 1) == 1

    @pl.when(i > 0)
    def _():
        stage(qk=(0, 0, False))

        def body(p, carry):
            stage(qk=(2 * p + 1, 1, False), sm=(2 * p, 0, False))
            stage(qk=(2 * p + 2, 0, False), sm=(2 * p + 1, 1, False))
            return carry

        lax.fori_loop(0, pairs, body, 0)

    @pl.when(odd)
    def _():
        stage(qk=(i, 1, True), sm=(i - 1, 0, False), last=(i, 1, True))

    @pl.when(jnp.logical_not(odd) & (i > 0))
    def _():
        stage(qk=(i - 1, 1, False), sm=(i - 2, 0, False))
        stage(qk=(i, 0, True), sm=(i - 1, 1, False), last=(i, 0, True))

    @pl.when(i == 0)
    def _():
        stage(qk=(0, 0, True), last=(0, 0, True))

    lam = lam_ref[...]
    lam_full = (jnp.exp(jnp.sum(lam[0:1] * lam[1:2], axis=-1, keepdims=True))
                - jnp.exp(jnp.sum(lam[2:3] * lam[3:4], axis=-1, keepdims=True)) + lam_init)
    for hh in range(nh):
        acc = acc_ref[hh]
        inv = 1.0 / acc[dv:dv + 1, :]
        ot = acc[0:dv, 0:tq] * inv[:, 0:tq] - lam_full * (acc[0:dv, tq:2 * tq] * inv[:, tq:2 * tq])
        o = ot.T
        o = o * lax.rsqrt(jnp.mean(o * o, axis=-1, keepdims=True) + EPS) * nw_ref[...] * (1.0 - lam_init)
        o_ref[:, hh * dv:(hh + 1) * dv] = (o * _silu(z_ref[hh].astype(F32))).astype(BF16)


def _diffattn(proj, lam, norm_w, lam_init):
    s = proj.shape[1]
    tq = 512
    nh = DA_HEADS_PER_STEP
    body = functools.partial(_diffattn_body, tq=tq, lam_init=lam_init)
    return pl.pallas_call(
        body,
        grid=(DA_HEADS // nh, s // tq),
        in_specs=[
            pl.BlockSpec((4, DA_QK_DIM), lambda h, i: (0, 0)),
            pl.BlockSpec((1, DA_V_DIM), lambda h, i: (0, 0)),
            pl.BlockSpec((nh, tq, LANES), lambda h, i: (CB_AQ // nh + h, i, 0)),
            pl.BlockSpec((nh, s, LANES), lambda h, i: (CB_AK // nh + h, 0, 0), pipeline_mode=pl.Buffered(1)),
            pl.BlockSpec((nh, s, LANES), lambda h, i: (CB_AV // nh + h, 0, 0), pipeline_mode=pl.Buffered(1)),
            pl.BlockSpec((nh, tq, LANES), lambda h, i: (CB_AZ // nh + h, i, 0)),
        ],
        out_specs=pl.BlockSpec((tq, nh * DA_V_DIM), lambda h, i: (i, h)),
        out_shape=jax.ShapeDtypeStruct((s, DA_HEADS * DA_V_DIM), BF16),
        scratch_shapes=[
            pltpu.VMEM((nh, 2 * tq, LANES), BF16),
            pltpu.VMEM((nh, s // tq, DA_V_DIM + DA_ONES_ROWS, tq), BF16),
            pltpu.VMEM((nh, 1, 2 * tq), F32),
            pltpu.VMEM((nh, DA_V_DIM + DA_ONES_ROWS, 2 * tq), F32),
            pltpu.VMEM((2, nh, tq, 2 * tq), BF16),
        ],
        compiler_params=_params("arbitrary", "arbitrary"),
        name="diffattn",
    )(lam, norm_w.reshape(1, DA_V_DIM), proj, proj, proj, proj)


def _retention_tables(s):
    c = RET_CHUNK
    half = RET_QK_DIM // 2
    theta = 1.0 / (ROT_BASE ** jnp.linspace(0.0, 1.0, half, dtype=F32))
    ang = jnp.arange(s, dtype=F32)[:, None] * theta[None, :]
    cos = jnp.tile(jnp.repeat(jnp.cos(ang), 2, axis=-1), (1, 2))
    sign = jnp.tile(jnp.asarray([-1.0, 1.0], F32), RET_QK_DIM)
    sin = jnp.tile(jnp.repeat(jnp.sin(ang), 2, axis=-1), (1, 2)) * sign[None, :]
    log_g = jnp.log1p(-jnp.exp2(-5.0 - jnp.arange(RET_HEADS, dtype=F32)))
    idx = jnp.arange(c, dtype=F32)
    lane_g = jnp.repeat(log_g.reshape(RET_HEADS // 2, 2), RET_QK_DIM, axis=-1)
    qdec = jnp.exp((idx + 1.0)[None, :, None] * lane_g[:, None, :])
    kdec = jnp.exp((c - 1.0 - idx)[None, :, None] * lane_g[:, None, :]) * RET_QK_DIM ** -0.5
    cdec = jnp.broadcast_to(jnp.exp(c * lane_g)[:, :, None], (RET_HEADS // 2, LANES, RET_V_DIM))
    causal = idx[:, None] >= idx[None, :]
    dmask = jnp.where(causal[None], jnp.exp(-c * log_g)[:, None, None], 0.0)
    return cos, sin, qdec, kdec, cdec, dmask.astype(F32)


def _retention_body(cos_ref, sin_ref, qdec_ref, kdec_ref, cdec_ref, dmask_ref, nw_ref,
                    q_ref, k_ref, v_ref, z_ref, o_ref, r_ref):
    @pl.when(pl.program_id(0) == 0)
    def _():
        r_ref[...] = jnp.zeros(r_ref.shape, F32)

    c = q_ref.shape[1]
    cos = cos_ref[...]
    sin = sin_ref[...]
    lane = lax.broadcasted_iota(jnp.int32, (c, LANES), 1)
    even = (lane & 1) == 0
    head0 = lane < RET_QK_DIM

    def rotate(x):
        nxt = pltpu.roll(x, LANES - 1, 1)
        prv = pltpu.roll(x, 1, 1)
        return x * cos + jnp.where(even, nxt, prv) * sin

    row = lax.broadcasted_iota(jnp.int32, r_ref.shape[1:], 0)
    for pp in range(q_ref.shape[0]):
        qd = (rotate(q_ref[pp].astype(F32)) * qdec_ref[pp]).astype(BF16)
        kd = (rotate(k_ref[pp].astype(F32)) * kdec_ref[pp]).astype(BF16)
        zero = jnp.zeros_like(qd)
        state = r_ref[pp]
        state_b = state.astype(BF16)
        updates = []
        for hh in range(2):
            head = 2 * pp + hh
            qz = jnp.where(head0, qd, zero) if hh == 0 else jnp.where(head0, zero, qd)
            sc = lax.dot_general(qz, kd, (((1,), (1,)), ((), ())), preferred_element_type=F32) * dmask_ref[head]
            v = v_ref[head]
            o = (jnp.dot(sc.astype(BF16), v, preferred_element_type=F32)
                 + jnp.dot(qz, state_b, preferred_element_type=F32))
            o = o * lax.rsqrt(jnp.mean(o * o, axis=-1, keepdims=True) + EPS) * nw_ref[...]
            o_ref[:, head * RET_V_DIM:(head + 1) * RET_V_DIM] = (o * _silu(z_ref[head].astype(F32))).astype(BF16)
            updates.append(lax.dot_general(kd, v, (((0,), (0,)), ((), ())), preferred_element_type=F32))
        r_ref[pp] = cdec_ref[pp] * state + jnp.where(row < RET_QK_DIM, updates[0], updates[1])


def _retention(proj, norm_w):
    s = proj.shape[1]
    c = RET_CHUNK
    pairs = RET_HEADS // 2
    cos, sin, qdec, kdec, cdec, dmask = _retention_tables(s)
    return pl.pallas_call(
        _retention_body,
        grid=(s // c,),
        in_specs=[
            pl.BlockSpec((c, LANES), lambda n: (n, 0)),
            pl.BlockSpec((c, LANES), lambda n: (n, 0)),
            pl.BlockSpec((pairs, c, LANES), lambda n: (0, 0, 0)),
            pl.BlockSpec((pairs, c, LANES), lambda n: (0, 0, 0)),
            pl.BlockSpec((pairs, LANES, RET_V_DIM), lambda n: (0, 0, 0)),
            pl.BlockSpec((RET_HEADS, c, c), lambda n: (0, 0, 0)),
            pl.BlockSpec((1, RET_V_DIM), lambda n: (0, 0)),
            pl.BlockSpec((pairs, c, LANES), lambda n: (CB_BQ // pairs, n, 0)),
            pl.BlockSpec((pairs, c, LANES), lambda n: (CB_BK // pairs, n, 0)),
            pl.BlockSpec((RET_HEADS, c, LANES), lambda n: (CB_BV // RET_HEADS, n, 0)),
            pl.BlockSpec((RET_HEADS, c, LANES), lambda n: (CB_BZ // RET_HEADS, n, 0)),
        ],
        out_specs=pl.BlockSpec((c, RET_HEADS * RET_V_DIM), lambda n: (n, 0)),
        out_shape=jax.ShapeDtypeStruct((s, RET_HEADS * RET_V_DIM), BF16),
        scratch_shapes=[pltpu.VMEM((pairs, LANES, RET_V_DIM), F32)],
        compiler_params=_params("arbitrary"),
        name="retention",
    )(cos, sin, qdec, kdec, cdec, dmask, norm_w.reshape(1, RET_V_DIM), proj, proj, proj, proj)


def _dilated_body(*refs):
    (q0, k0, v0, q1, k1, v1, q2, k2, v2, z_ref, o_ref,
     slab, qp, kp0, vp0, kp1, vp1, kp2, vp2, og0, og1, og2, ls0, ls1, ls2) = refs
    t = pl.program_id(1)
    blk = DIL_RADIUS
    groups = (
        (1, q0, k0, v0, kp0, vp0, og0, ls0),
        (4, q1, k1, v1, kp1, vp1, og1, ls1),
        (16, q2, k2, v2, kp2, vp2, og2, ls2),
    )

    qi = lax.broadcasted_iota(jnp.int32, (blk, 2 * blk), 0)
    kj = lax.broadcasted_iota(jnp.int32, (blk, 2 * blk), 1)
    in_window = (kj >= qi) & (kj <= qi + blk)
    bias = jnp.where(in_window, 0.0, MASK_VALUE).astype(F32)
    bias_first = jnp.where(in_window & (kj >= blk), 0.0, MASK_VALUE).astype(F32)

    for d, q_ref, k_ref, v_ref, kp, vp, og, ls in groups:
        m_t = DIL_TILE // d
        nb = m_t // blk

        @pl.when(t == 0)
        def _():
            kp[:, 0:blk, :] = jnp.zeros((d, blk, LANES), BF16)
            vp[:, 0:blk, 0:LANES] = jnp.zeros((d, blk, LANES), BF16)
            vp[:, :, LANES:2 * LANES] = jnp.ones((d, blk + m_t, LANES), BF16)

        if d == 1:
            qp[...] = q_ref[0]
            kp[0, blk:blk + m_t, :] = k_ref[0]
            vp[0, blk:blk + m_t, 0:LANES] = v_ref[0]
        else:
            slab[...] = q_ref[0].astype(F32)
            for r in range(d):
                qp[r * m_t:(r + 1) * m_t, :] = slab[pl.ds(r, m_t, stride=d), :].astype(BF16)
            slab[...] = k_ref[0].astype(F32)
            for r in range(d):
                kp[r, blk:blk + m_t, :] = slab[pl.ds(r, m_t, stride=d), :].astype(BF16)
            slab[...] = v_ref[0].astype(F32)
            for r in range(d):
                vp[r, blk:blk + m_t, 0:LANES] = slab[pl.ds(r, m_t, stride=d), :].astype(BF16)

        def unit(u, carry, d=d, m_t=m_t, nb=nb, kp=kp, vp=vp, og=og, ls=ls):
            r = u // nb
            b = u % nb
            qu = qp[pl.ds(pl.multiple_of(r * m_t + b * blk, blk), blk), :]
            start = pl.multiple_of(b * blk, blk)
            kw = kp[r, pl.ds(start, 2 * blk), :]
            vw = vp[r, pl.ds(start, 2 * blk), :]
            s = lax.dot_general(qu, kw, (((1,), (1,)), ((), ())), preferred_element_type=F32)
            s = s + jnp.where((t == 0) & (b == 0), bias_first, bias)
            m = jnp.max(s, axis=-1, keepdims=True)
            p = jnp.exp2(s - m)
            pv = jnp.dot(p.astype(BF16), vw, preferred_element_type=F32)
            l = pv[:, LANES:2 * LANES]
            o = pv[:, 0:LANES] / l
            lse = m + jnp.log2(l)
            if d == 1:
                row0 = pl.multiple_of(b * blk, blk)
                og[pl.ds(row0, blk), :] = o
                ls[pl.ds(row0, blk), :] = lse
            else:
                row0 = b * (blk * d) + r
                og[pl.ds(row0, blk, stride=d), :] = o
                ls[pl.ds(row0, blk, stride=d), :] = lse
            return carry

        lax.fori_loop(0, d * nb, unit, 0, unroll=DIL_UNROLL)

        kp[:, 0:blk, :] = kp[:, m_t:m_t + blk, :]
        vp[:, 0:blk, 0:LANES] = vp[:, m_t:m_t + blk, 0:LANES]

    l0, l1, l2 = ls0[...], ls1[...], ls2[...]
    mx = jnp.maximum(jnp.maximum(l0, l1), l2)
    w0, w1, w2 = jnp.exp2(l0 - mx), jnp.exp2(l1 - mx), jnp.exp2(l2 - mx)
    o = (w0 * og0[...] + w1 * og1[...] + w2 * og2[...]) / (w0 + w1 + w2)
    o_ref[...] = (o * _silu(z_ref[0].astype(F32))).astype(BF16)


def _dilated(proj):
    s = proj.shape[1]
    tile = DIL_TILE
    blk = DIL_RADIUS

    def qkv_spec(base, g):
        return pl.BlockSpec((1, tile, LANES), lambda h, t: (base + g * DIL_HEADS + h, t, 0))

    in_specs = []
    for g in range(len(DIL_DILATIONS)):
        in_specs += [qkv_spec(CB_CQ, g), qkv_spec(CB_CK, g), qkv_spec(CB_CV, g)]
    in_specs.append(pl.BlockSpec((1, tile, LANES), lambda h, t: (CB_CZ + h, t, 0)))
    scratch = [pltpu.VMEM((tile, LANES), F32), pltpu.VMEM((tile, LANES), BF16)]
    for d in DIL_DILATIONS:
        scratch += [pltpu.VMEM((d, blk + tile // d, LANES), BF16), pltpu.VMEM((d, blk + tile // d, 2 * LANES), BF16)]
    scratch += [pltpu.VMEM((tile, LANES), F32)] * 6
    return pl.pallas_call(
        _dilated_body,
        grid=(DIL_HEADS, s // tile),
        in_specs=in_specs,
        out_specs=pl.BlockSpec((tile, DIL_HEAD_DIM), lambda h, t: (t, h)),
        out_shape=jax.ShapeDtypeStruct((s, DIL_HEADS * DIL_HEAD_DIM), BF16),
        scratch_shapes=scratch,
        compiler_params=_params("arbitrary", "arbitrary"),
        name="dilated",
    )(*([proj] * 10))


def _merge_body(*refs, final):
    (ga_ref, gb_ref, gc_ref, ta0, ta1, tb0, tb1, tc0, tc1, pa_ref, pb_ref, pc_ref, wo_ref,
     h_ref, gate_ref, nw_ref) = refs[:16]

    def branch(g_ref, t_refs, p_ref):
        y = jnp.dot(g_ref[...], p_ref[...], preferred_element_type=F32)
        gate = jnp.concatenate([t[cblk] for t in t_refs for cblk in range(t.shape[0])], axis=-1).astype(F32)
        return jax.nn.sigmoid(gate) * y

    merged = (branch(ga_ref, (ta0, ta1), pa_ref) + branch(gb_ref, (tb0, tb1), pb_ref)
              + branch(gc_ref, (tc0, tc1), pc_ref))
    y = jnp.dot(merged.astype(BF16), wo_ref[...], preferred_element_type=F32)
    out = h_ref[...] + gate_ref[0] * y
    if final:
        (o_ref,) = refs[16:]
        o_ref[...] = out * lax.rsqrt(jnp.mean(out * out, axis=-1, keepdims=True) + EPS) * nw_ref[...]
    else:
        shift_ref, scale_ref, o_ref, u_ref = refs[16:]
        o_ref[...] = out
        u_ref[...] = _modulated_norm(out, nw_ref[...], scale_ref[0], shift_ref[0])


def _merge(ga, gb, gc, proj, pa, pb, pc, wo, h, mod, layer, tail_norm_w, final):
    s, d = h.shape
    w = ga.shape[1]
    tm = 256
    const = dict(pipeline_mode=pl.Buffered(1))
    row_spec = pl.BlockSpec((tm, d), lambda i: (i, 0))
    tail_specs, tail_args = [], []
    out_specs, out_shape = row_spec, jax.ShapeDtypeStruct((s, d), F32)
    if not final:
        tail_specs = [pl.BlockSpec((1, 1, d), lambda i: (layer + 1, 0, 0)),
                      pl.BlockSpec((1, 1, d), lambda i: (layer + 1, 0, 1))]
        tail_args = [mod, mod]
        out_specs, out_shape = [row_spec, row_spec], [out_shape, jax.ShapeDtypeStruct((s, d), BF16)]
    return pl.pallas_call(
        functools.partial(_merge_body, final=final),
        grid=(s // tm,),
        out_specs=out_specs,
        out_shape=out_shape,
        in_specs=[
            pl.BlockSpec((tm, w), lambda i: (i, 0)),
            pl.BlockSpec((tm, w), lambda i: (i, 0)),
            pl.BlockSpec((tm, w), lambda i: (i, 0)),
            *[pl.BlockSpec((GATE_CB, tm, LANES), functools.partial(lambda i, b: (b, i, 0), b=CB_GA // GATE_CB + b))
              for b in range(3 * d // LANES // GATE_CB)],
            pl.BlockSpec((w, d), lambda i: (0, 0), **const),
            pl.BlockSpec((w, d), lambda i: (0, 0), **const),
            pl.BlockSpec((w, d), lambda i: (0, 0), **const),
            pl.BlockSpec((d, d), lambda i: (0, 0), **const),
            row_spec,
            pl.BlockSpec((1, 1, d), lambda i: (layer, 0, 2)),
            pl.BlockSpec((1, d), lambda i: (0, 0)),
            *tail_specs,
        ],
        compiler_params=_params("arbitrary"),
        name="merge",
    )(ga, gb, gc, *([proj] * 6), pa, pb, pc, wo, h, mod, tail_norm_w.reshape(1, d), *tail_args)


def kernel(x, c, w_in, w_proj_a, w_proj_b, w_proj_c, w_out, w_ada, b_ada, norm_w, lam, da_norm_w, ret_norm_w,
           final_norm_w):
    batch, s, d = x.shape
    assert batch == 1 and d == D_MODEL and s % DIL_TILE == 0 and w_in.shape[2] == IN_WIDTH
    depth = w_in.shape[0]
    mod = _adaln(c, w_ada, b_ada)
    h = x[0]
    u = _modnorm(h, norm_w[0], mod, 0)
    for layer in range(depth):
        final = layer == depth - 1
        lam_init = 0.8 - 0.6 * math.exp(-0.3 * layer)
        proj = _inproj(u, layer, w_in)
        ga = _diffattn(proj, lam[layer], da_norm_w[layer], lam_init)
        gb = _retention(proj, ret_norm_w[layer])
        gc = _dilated(proj)
        res = _merge(ga, gb, gc, proj,
                     w_proj_a[layer].astype(BF16), w_proj_b[layer].astype(BF16), w_proj_c[layer].astype(BF16),
                     w_out[layer].astype(BF16), h, mod, layer,
                     final_norm_w if final else norm_w[layer + 1], final)
        if final:
            return res[None]
        h, u = res
```

```python
import functools
import math

import numpy as np
import jax
import jax.numpy as jnp
from jax import lax
from jax.experimental import pallas as pl
from jax.experimental.pallas import tpu as pltpu

F32 = jnp.float32
BF16 = jnp.bfloat16

LANES = 128
VMEM_LIMIT_BYTES = 56 * 1024 * 1024

EPS = 1e-6
LOG2E = math.log2(math.e)

D_MODEL = 2048
DA_HEADS = 8
DA_QK_DIM = 64
DA_V_DIM = 128
RET_HEADS = 8
RET_QK_DIM = 64
RET_V_DIM = 128
RET_CHUNK = 256
ROT_BASE = 10000.0
DIL_DILATIONS = (1, 4, 16)
DIL_RADIUS = 128
DIL_HEADS = 8
DIL_HEAD_DIM = 128
DIL_TILE = 2048
DIL_SPLIT_STRIDE = 4
DIL_UNROLL = 16
MASK_VALUE = -1e30

CB_AQ, CB_AK, CB_AV, CB_AZ = 0, 8, 16, 24
CB_BQ, CB_BK, CB_BV, CB_BZ = 32, 36, 40, 48
CB_CQ, CB_CK, CB_CV, CB_CZ = 56, 80, 104, 128
CB_GA, CB_GB, CB_GC = 136, 152, 168
GATE_CB = 8
IN_WIDTH = 184 * LANES


def _params(*sem, flags=None):
    return pltpu.CompilerParams(dimension_semantics=sem, vmem_limit_bytes=VMEM_LIMIT_BYTES, flags=flags)


def _silu(z):
    return z * jax.nn.sigmoid(z)


def _adaln_body(c_ref, w_ref, b_ref, o_ref):
    cc = c_ref[...]
    o_ref[0] = jnp.sum(_silu(cc) * w_ref[0], axis=0, keepdims=True) + b_ref[0]


def _adaln(c, w_ada, b_ada):
    n_layers, d, e = w_ada.shape
    tn = 768
    return pl.pallas_call(
        _adaln_body,
        grid=(n_layers, e // tn),
        in_specs=[
            pl.BlockSpec((d, 1), lambda l, j: (0, 0)),
            pl.BlockSpec((1, d, tn), lambda l, j: (l, 0, j)),
            pl.BlockSpec((1, 1, tn), lambda l, j: (l, 0, j)),
        ],
        out_specs=pl.BlockSpec((1, 1, tn), lambda l, j: (l, 0, j)),
        out_shape=jax.ShapeDtypeStruct((n_layers, 1, e), F32),
        compiler_params=_params("arbitrary", "arbitrary"),
        name="adaln",
    )(c.reshape(d, 1), w_ada, b_ada.reshape(n_layers, 1, e))


def _modulated_norm(xf, nw, scale, shift):
    y = xf * lax.rsqrt(jnp.mean(xf * xf, axis=-1, keepdims=True) + EPS) * nw
    return (y * (1.0 + scale) + shift).astype(BF16)


def _modnorm_body(x_ref, nw_ref, shift_ref, scale_ref, u_ref):
    u_ref[...] = _modulated_norm(x_ref[...], nw_ref[...], scale_ref[0], shift_ref[0])


def _modnorm(x, norm_w, mod, layer):
    s, d = x.shape
    tm = 512
    return pl.pallas_call(
        _modnorm_body,
        grid=(s // tm,),
        in_specs=[
            pl.BlockSpec((tm, d), lambda i: (i, 0)),
            pl.BlockSpec((1, d), lambda i: (0, 0)),
            pl.BlockSpec((1, 1, d), lambda i: (layer, 0, 0)),
            pl.BlockSpec((1, 1, d), lambda i: (layer, 0, 1)),
        ],
        out_specs=pl.BlockSpec((tm, d), lambda i: (i, 0)),
        out_shape=jax.ShapeDtypeStruct((s, d), BF16),
        compiler_params=_params("arbitrary"),
        name="modnorm",
    )(x, norm_w.reshape(1, d), mod, mod)


def _inproj_body(u_ref, w_ref, o_ref, *, tn, a_scale, c_scale):
    j = pl.program_id(1)
    acc = jnp.dot(u_ref[...], w_ref[...].astype(BF16), preferred_element_type=F32)
    cb = j * (tn // LANES)
    qscale = jnp.where(cb < CB_AK, a_scale, jnp.where((cb >= CB_CQ) & (cb < CB_CK), c_scale, 1.0))
    acc = acc * qscale
    for cblk in range(tn // LANES):
        o_ref[cblk] = acc[:, cblk * LANES:(cblk + 1) * LANES].astype(BF16)


def _inproj(u, layer, w_in):
    s, d = u.shape
    tm, tn = 2048, 1024
    body = functools.partial(
        _inproj_body, tn=tn,
        a_scale=DA_QK_DIM ** -0.5 * LOG2E, c_scale=DIL_HEAD_DIM ** -0.5 * LOG2E)
    return pl.pallas_call(
        body,
        grid=(s // tm, IN_WIDTH // tn),
        in_specs=[
            pl.BlockSpec((tm, d), lambda i, j: (i, 0)),
            pl.BlockSpec((None, d, tn), lambda i, j: (layer, 0, j)),
        ],
        out_specs=pl.BlockSpec((tn // LANES, tm, LANES), lambda i, j: (j, i, 0)),
        out_shape=jax.ShapeDtypeStruct((IN_WIDTH // LANES, s, LANES), BF16),
        compiler_params=_params("arbitrary", "arbitrary"),
        name="inproj",
    )(u, w_in)


DA_HEADS_PER_STEP = 4
DA_SKEW = 4
DA_QCHUNK = 256
DA_ONES_ROWS = 16


def _diffattn_body(lam_ref, nw_ref, q_ref, k_ref, v_ref, z_ref, o_ref, qs_ref, vt_ref, m_ref, acc_ref,
                   *, tq, lam_init):
    i = pl.program_id(1)
    nh = q_ref.shape[0]
    dv = DA_V_DIM

    lane = lax.broadcasted_iota(jnp.int32, (tq, LANES), 1)
    for hh in range(nh):
        q = q_ref[hh]
        zero = jnp.zeros_like(q)
        qs_ref[hh, 0:tq] = jnp.where(lane < DA_QK_DIM, q, zero)
        qs_ref[hh, tq:2 * tq] = jnp.where(lane >= DA_QK_DIM, q, zero)
    m_ref[...] = jnp.full(m_ref.shape, -jnp.inf, F32)
    acc_ref[...] = jnp.zeros(acc_ref.shape, F32)

    def step(kt, diagonal):
        start = pl.multiple_of(kt * tq, tq)
        chains = [(hh, c * DA_QCHUNK) for hh in range(nh) for c in range(2 * tq // DA_QCHUNK)]
        def visible_keys(c0):
            return min(tq, c0 % tq + DA_QCHUNK) if diagonal else tq

        def qk(hh, c0):
            k = k_ref[hh, pl.ds(start, visible_keys(c0)), :]
            st = lax.dot_general(k, qs_ref[hh, c0:c0 + DA_QCHUNK, :], (((1,), (1,)), ((), ())),
                                 preferred_element_type=F32)
            if diagonal:
                key = lax.broadcasted_iota(jnp.int32, st.shape, 0)
                qpos = lax.broadcasted_iota(jnp.int32, st.shape, 1) + (c0 % tq)
                st = jnp.where(key <= qpos, st, -jnp.inf)
            return st.astype(BF16)

        def softmax(hh, c0, st):
            m_prev = m_ref[hh, :, c0:c0 + DA_QCHUNK]
            m_new = jnp.maximum(m_prev, jnp.max(st, axis=0, keepdims=True).astype(F32))
            m_ref[hh, :, c0:c0 + DA_QCHUNK] = m_new
            return jnp.exp2(m_prev - m_new), jnp.exp2(st - m_new.astype(BF16))

        def pv(hh, c0, alpha, pt):
            vt = vt_ref[hh, kt, :, 0:visible_keys(c0)]
            acc_ref[hh, :, c0:c0 + DA_QCHUNK] = (alpha * acc_ref[hh, :, c0:c0 + DA_QCHUNK]
                                                 + jnp.dot(vt, pt, preferred_element_type=F32))

        scores = {}
        for n in range(len(chains) + DA_SKEW):
            if n < len(chains):
                scores[n] = qk(*chains[n])
            if n >= DA_SKEW:
                ch = chains[n - DA_SKEW]
                pv(*ch, *softmax(*ch, scores.pop(n - DA_SKEW)))

    def off_diagonal(kt, carry):
        step(kt, False)
        return carry

    lax.fori_loop(0, i, off_diagonal, 0)
    diag_start = pl.multiple_of(i * tq, tq)
    for hh in range(nh):
        vt_ref[hh, i, 0:dv, :] = v_ref[hh, pl.ds(diag_start, tq), :].astype(F32).T.astype(BF16)
        vt_ref[hh, i, dv:dv + DA_ONES_ROWS, :] = jnp.ones((DA_ONES_ROWS, tq), BF16)
    step(i, True)

    lam = lam_ref[...]
    lam_full = (jnp.exp(jnp.sum(lam[0:1] * lam[1:2], axis=-1, keepdims=True))
                - jnp.exp(jnp.sum(lam[2:3] * lam[3:4], axis=-1, keepdims=True)) + lam_init)
    for hh in range(nh):
        acc = acc_ref[hh]
        inv = 1.0 / acc[dv:dv + 1, :]
        ot = acc[0:dv, 0:tq] * inv[:, 0:tq] - lam_full * (acc[0:dv, tq:2 * tq] * inv[:, tq:2 * tq])
        o = ot.T
        o = o * lax.rsqrt(jnp.mean(o * o, axis=-1, keepdims=True) + EPS) * nw_ref[...] * (1.0 - lam_init)
        o_ref[:, hh * dv:(hh + 1) * dv] = (o * _silu(z_ref[hh].astype(F32))).astype(BF16)


def _diffattn(proj, lam, norm_w, lam_init):
    s = proj.shape[1]
    tq = 512
    nh = DA_HEADS_PER_STEP
    body = functools.partial(_diffattn_body, tq=tq, lam_init=lam_init)
    return pl.pallas_call(
        body,
        grid=(DA_HEADS // nh, s // tq),
        in_specs=[
            pl.BlockSpec((4, DA_QK_DIM), lambda h, i: (0, 0)),
            pl.BlockSpec((1, DA_V_DIM), lambda h, i: (0, 0)),
            pl.BlockSpec((nh, tq, LANES), lambda h, i: (CB_AQ // nh + h, i, 0)),
            pl.BlockSpec((nh, s, LANES), lambda h, i: (CB_AK // nh + h, 0, 0)),
            pl.BlockSpec((nh, s, LANES), lambda h, i: (CB_AV // nh + h, 0, 0)),
            pl.BlockSpec((nh, tq, LANES), lambda h, i: (CB_AZ // nh + h, i, 0)),
        ],
        out_specs=pl.BlockSpec((tq, nh * DA_V_DIM), lambda h, i: (i, h)),
        out_shape=jax.ShapeDtypeStruct((s, DA_HEADS * DA_V_DIM), BF16),
        scratch_shapes=[
            pltpu.VMEM((nh, 2 * tq, LANES), BF16),
            pltpu.VMEM((nh, s // tq, DA_V_DIM + DA_ONES_ROWS, tq), BF16),
            pltpu.VMEM((nh, 1, 2 * tq), F32),
            pltpu.VMEM((nh, DA_V_DIM + DA_ONES_ROWS, 2 * tq), F32),
        ],
        compiler_params=_params("arbitrary", "arbitrary"),
        name="diffattn",
    )(lam, norm_w.reshape(1, DA_V_DIM), proj, proj, proj, proj)


def _retention_tables(s):
    c = RET_CHUNK
    half = RET_QK_DIM // 2
    theta = 1.0 / (ROT_BASE ** jnp.linspace(0.0, 1.0, half, dtype=F32))
    ang = jnp.arange(s, dtype=F32)[:, None] * theta[None, :]
    cos = jnp.tile(jnp.repeat(jnp.cos(ang), 2, axis=-1), (1, 2))
    sign = jnp.tile(jnp.asarray([-1.0, 1.0], F32), RET_QK_DIM)
    sin = jnp.tile(jnp.repeat(jnp.sin(ang), 2, axis=-1), (1, 2)) * sign[None, :]
    log_g = jnp.log1p(-jnp.exp2(-5.0 - jnp.arange(RET_HEADS, dtype=F32)))
    idx = jnp.arange(c, dtype=F32)
    lane_g = jnp.repeat(log_g.reshape(RET_HEADS // 2, 2), RET_QK_DIM, axis=-1)
    qdec = jnp.exp((idx + 1.0)[None, :, None] * lane_g[:, None, :])
    kdec = jnp.exp((c - 1.0 - idx)[None, :, None] * lane_g[:, None, :]) * RET_QK_DIM ** -0.5
    cdec = jnp.broadcast_to(jnp.exp(c * lane_g)[:, :, None], (RET_HEADS // 2, LANES, RET_V_DIM))
    causal = idx[:, None] >= idx[None, :]
    dmask = jnp.where(causal[None], jnp.exp(-c * log_g)[:, None, None], 0.0)
    return cos, sin, qdec, kdec, cdec, dmask.astype(F32)


def _retention_body(cos_ref, sin_ref, qdec_ref, kdec_ref, cdec_ref, dmask_ref, nw_ref,
                    q_ref, k_ref, v_ref, z_ref, o_ref, r_ref):
    @pl.when(pl.program_id(0) == 0)
    def _():
        r_ref[...] = jnp.zeros(r_ref.shape, F32)

    c = q_ref.shape[1]
    cos = cos_ref[...]
    sin = sin_ref[...]
    lane = lax.broadcasted_iota(jnp.int32, (c, LANES), 1)
    even = (lane & 1) == 0
    head0 = lane < RET_QK_DIM

    def rotate(x):
        nxt = pltpu.roll(x, LANES - 1, 1)
        prv = pltpu.roll(x, 1, 1)
        return x * cos + jnp.where(even, nxt, prv) * sin

    row = lax.broadcasted_iota(jnp.int32, r_ref.shape[1:], 0)
    for pp in range(q_ref.shape[0]):
        qd = (rotate(q_ref[pp].astype(F32)) * qdec_ref[pp]).astype(BF16)
        kd = (rotate(k_ref[pp].astype(F32)) * kdec_ref[pp]).astype(BF16)
        zero = jnp.zeros_like(qd)
        state = r_ref[pp]
        state_b = state.astype(BF16)
        updates = []
        for hh in range(2):
            head = 2 * pp + hh
            qz = jnp.where(head0, qd, zero) if hh == 0 else jnp.where(head0, zero, qd)
            sc = lax.dot_general(qz, kd, (((1,), (1,)), ((), ())), preferred_element_type=F32) * dmask_ref[head]
            v = v_ref[head]
            o = (jnp.dot(sc.astype(BF16), v, preferred_element_type=F32)
                 + jnp.dot(qz, state_b, preferred_element_type=F32))
            o = o * lax.rsqrt(jnp.mean(o * o, axis=-1, keepdims=True) + EPS) * nw_ref[...]
            o_ref[:, head * RET_V_DIM:(head + 1) * RET_V_DIM] = (o * _silu(z_ref[head].astype(F32))).astype(BF16)
            updates.append(lax.dot_general(kd, v, (((0,), (0,)), ((), ())), preferred_element_type=F32))
        r_ref[pp] = cdec_ref[pp] * state + jnp.where(row < RET_QK_DIM, updates[0], updates[1])


def _retention(proj, norm_w):
    s = proj.shape[1]
    c = RET_CHUNK
    pairs = RET_HEADS // 2
    cos, sin, qdec, kdec, cdec, dmask = _retention_tables(s)
    return pl.pallas_call(
        _retention_body,
        grid=(s // c,),
        in_specs=[
            pl.BlockSpec((c, LANES), lambda n: (n, 0)),
            pl.BlockSpec((c, LANES), lambda n: (n, 0)),
            pl.BlockSpec((pairs, c, LANES), lambda n: (0, 0, 0)),
            pl.BlockSpec((pairs, c, LANES), lambda n: (0, 0, 0)),
            pl.BlockSpec((pairs, LANES, RET_V_DIM), lambda n: (0, 0, 0)),
            pl.BlockSpec((RET_HEADS, c, c), lambda n: (0, 0, 0)),
            pl.BlockSpec((1, RET_V_DIM), lambda n: (0, 0)),
            pl.BlockSpec((pairs, c, LANES), lambda n: (CB_BQ // pairs, n, 0)),
            pl.BlockSpec((pairs, c, LANES), lambda n: (CB_BK // pairs, n, 0)),
            pl.BlockSpec((RET_HEADS, c, LANES), lambda n: (CB_BV // RET_HEADS, n, 0)),
            pl.BlockSpec((RET_HEADS, c, LANES), lambda n: (CB_BZ // RET_HEADS, n, 0)),
        ],
        out_specs=pl.BlockSpec((c, RET_HEADS * RET_V_DIM), lambda n: (n, 0)),
        out_shape=jax.ShapeDtypeStruct((s, RET_HEADS * RET_V_DIM), BF16),
        scratch_shapes=[pltpu.VMEM((pairs, LANES, RET_V_DIM), F32)],
        compiler_params=_params("arbitrary"),
        name="retention",
    )(cos, sin, qdec, kdec, cdec, dmask, norm_w.reshape(1, RET_V_DIM), proj, proj, proj, proj)


def _dilated_body(*refs):
    (q0, k0, v0, q1, k1, v1, q2, k2, v2, z_ref, o_ref,
     slab, slab2, qp, kp0, vp0, kp1, vp1, kp2, vp2, og0, og1, og2, ls0, ls1, ls2) = refs
    t = pl.program_id(1)
    blk = DIL_RADIUS
    groups = (
        (1, q0, k0, v0, kp0, vp0, og0, ls0),
        (4, q1, k1, v1, kp1, vp1, og1, ls1),
        (16, q2, k2, v2, kp2, vp2, og2, ls2),
    )

    qi = lax.broadcasted_iota(jnp.int32, (blk, 2 * blk), 0)
    kj = lax.broadcasted_iota(jnp.int32, (blk, 2 * blk), 1)
    in_window = (kj >= qi) & (kj <= qi + blk)
    bias = jnp.where(in_window, 0.0, MASK_VALUE).astype(F32)
    bias_first = jnp.where(in_window & (kj >= blk), 0.0, MASK_VALUE).astype(F32)

    for d, q_ref, k_ref, v_ref, kp, vp, og, ls in groups:
        m_t = DIL_TILE // d
        nb = m_t // blk

        @pl.when(t == 0)
        def _():
            kp[:, 0:blk, :] = jnp.zeros((d, blk, LANES), BF16)
            vp[:, 0:blk, 0:LANES] = jnp.zeros((d, blk, LANES), BF16)
            vp[:, :, LANES:2 * LANES] = jnp.ones((d, blk + m_t, LANES), BF16)

        if d == 1:
            qp[...] = q_ref[0]
            kp[0, blk:blk + m_t, :] = k_ref[0]
            vp[0, blk:blk + m_t, 0:LANES] = v_ref[0]
        else:
            def residue_rows(x_ref, d=d, m_t=m_t):
                slab[...] = x_ref[0].astype(F32)
                if d == DIL_SPLIT_STRIDE ** 2:
                    s4 = DIL_SPLIT_STRIDE
                    quarter = DIL_TILE // s4
                    for r0 in range(s4):
                        slab2[r0 * quarter:(r0 + 1) * quarter, :] = slab[pl.ds(r0, quarter, stride=s4), :]
                    return [slab2[pl.ds((r % s4) * quarter + r // s4, m_t, stride=s4), :] for r in range(d)]
                return [slab[pl.ds(r, m_t, stride=d), :] for r in range(d)]

            for r, rows in enumerate(residue_rows(q_ref)):
                qp[r * m_t:(r + 1) * m_t, :] = rows.astype(BF16)
            for r, rows in enumerate(residue_rows(k_ref)):
                kp[r, blk:blk + m_t, :] = rows.astype(BF16)
            for r, rows in enumerate(residue_rows(v_ref)):
                vp[r, blk:blk + m_t, 0:LANES] = rows.astype(BF16)

        def unit(u, carry, d=d, m_t=m_t, nb=nb, kp=kp, vp=vp, og=og, ls=ls):
            r = u // nb
            b = u % nb
            qu = qp[pl.ds(pl.multiple_of(r * m_t + b * blk, blk), blk), :]
            start = pl.multiple_of(b * blk, blk)
            kw = kp[r, pl.ds(start, 2 * blk), :]
            vw = vp[r, pl.ds(start, 2 * blk), :]
            s = lax.dot_general(qu, kw, (((1,), (1,)), ((), ())), preferred_element_type=F32)
            s = s + jnp.where((t == 0) & (b == 0), bias_first, bias)
            m = jnp.max(s, axis=-1, keepdims=True)
            p = jnp.exp2(s - m)
            pv = jnp.dot(p.astype(BF16), vw, preferred_element_type=F32)
            l = pv[:, LANES:2 * LANES]
            o = pv[:, 0:LANES] / l
            lse = m + jnp.log2(l)
            if d == 1:
                row0 = pl.multiple_of(b * blk, blk)
                og[pl.ds(row0, blk), :] = o
                ls[pl.ds(row0, blk), :] = lse
            elif d == DIL_SPLIT_STRIDE ** 2:
                s4 = DIL_SPLIT_STRIDE
                row0 = (r % s4) * (DIL_TILE // s4) + b * (blk * d // s4) + r // s4
                slab[pl.ds(row0, blk, stride=d // s4), :] = o
                slab2[pl.ds(row0, blk, stride=d // s4), :] = lse
            else:
                row0 = b * (blk * d) + r
                og[pl.ds(row0, blk, stride=d), :] = o
                ls[pl.ds(row0, blk, stride=d), :] = lse
            return carry

        lax.fori_loop(0, d * nb, unit, 0, unroll=DIL_UNROLL)
        if d == DIL_SPLIT_STRIDE ** 2:
            quarter = DIL_TILE // DIL_SPLIT_STRIDE
            for r0 in range(DIL_SPLIT_STRIDE):
                og[pl.ds(r0, quarter, stride=DIL_SPLIT_STRIDE), :] = slab[r0 * quarter:(r0 + 1) * quarter, :]
                ls[pl.ds(r0, quarter, stride=DIL_SPLIT_STRIDE), :] = slab2[r0 * quarter:(r0 + 1) * quarter, :]

        kp[:, 0:blk, :] = kp[:, m_t:m_t + blk, :]
        vp[:, 0:blk, 0:LANES] = vp[:, m_t:m_t + blk, 0:LANES]

    l0, l1, l2 = ls0[...], ls1[...], ls2[...]
    mx = jnp.maximum(jnp.maximum(l0, l1), l2)
    w0, w1, w2 = jnp.exp2(l0 - mx), jnp.exp2(l1 - mx), jnp.exp2(l2 - mx)
    o = (w0 * og0[...] + w1 * og1[...] + w2 * og2[...]) / (w0 + w1 + w2)
    o_ref[...] = (o * _silu(z_ref[0].astype(F32))).astype(BF16)


def _dilated(proj):
    s = proj.shape[1]
    tile = DIL_TILE
    blk = DIL_RADIUS

    def qkv_spec(base, g):
        return pl.BlockSpec((1, tile, LANES), lambda h, t: (base + g * DIL_HEADS + h, t, 0))

    in_specs = []
    for g in range(len(DIL_DILATIONS)):
        in_specs += [qkv_spec(CB_CQ, g), qkv_spec(CB_CK, g), qkv_spec(CB_CV, g)]
    in_specs.append(pl.BlockSpec((1, tile, LANES), lambda h, t: (CB_CZ + h, t, 0)))
    scratch = [pltpu.VMEM((tile, LANES), F32), pltpu.VMEM((tile, LANES), F32), pltpu.VMEM((tile, LANES), BF16)]
    for d in DIL_DILATIONS:
        scratch += [pltpu.VMEM((d, blk + tile // d, LANES), BF16), pltpu.VMEM((d, blk + tile // d, 2 * LANES), BF16)]
    scratch += [pltpu.VMEM((tile, LANES), F32)] * 6
    return pl.pallas_call(
        _dilated_body,
        grid=(DIL_HEADS, s // tile),
        in_specs=in_specs,
        out_specs=pl.BlockSpec((tile, DIL_HEAD_DIM), lambda h, t: (t, h)),
        out_shape=jax.ShapeDtypeStruct((s, DIL_HEADS * DIL_HEAD_DIM), BF16),
        scratch_shapes=scratch,
        compiler_params=_params("arbitrary", "arbitrary"),
        name="dilated",
    )(*([proj] * 10))


def _merge_body(*refs, final):
    (ga_ref, gb_ref, gc_ref, ta0, ta1, tb0, tb1, tc0, tc1, pa_ref, pb_ref, pc_ref, wo_ref,
     h_ref, gate_ref, nw_ref) = refs[:16]

    def branch(g_ref, t_refs, p_ref):
        y = jnp.dot(g_ref[...], p_ref[...], preferred_element_type=F32)
        gate = jnp.concatenate([t[cblk] for t in t_refs for cblk in range(t.shape[0])], axis=-1).astype(F32)
        return jax.nn.sigmoid(gate) * y

    merged = (branch(ga_ref, (ta0, ta1), pa_ref) + branch(gb_ref, (tb0, tb1), pb_ref)
              + branch(gc_ref, (tc0, tc1), pc_ref))
    y = jnp.dot(merged.astype(BF16), wo_ref[...], preferred_element_type=F32)
    out = h_ref[...] + gate_ref[0] * y
    if final:
        (o_ref,) = refs[16:]
        o_ref[...] = out * lax.rsqrt(jnp.mean(out * out, axis=-1, keepdims=True) + EPS) * nw_ref[...]
    else:
        shift_ref, scale_ref, o_ref, u_ref = refs[16:]
        o_ref[...] = out
        u_ref[...] = _modulated_norm(out, nw_ref[...], scale_ref[0], shift_ref[0])


def _merge(ga, gb, gc, proj, pa, pb, pc, wo, h, mod, layer, tail_norm_w, final):
    s, d = h.shape
    w = ga.shape[1]
    tm = 256
    const = dict(pipeline_mode=pl.Buffered(1))
    row_spec = pl.BlockSpec((tm, d), lambda i: (i, 0))
    tail_specs, tail_args = [], []
    out_specs, out_shape = row_spec, jax.ShapeDtypeStruct((s, d), F32)
    if not final:
        tail_specs = [pl.BlockSpec((1, 1, d), lambda i: (layer + 1, 0, 0)),
                      pl.BlockSpec((1, 1, d), lambda i: (layer + 1, 0, 1))]
        tail_args = [mod, mod]
        out_specs, out_shape = [row_spec, row_spec], [out_shape, jax.ShapeDtypeStruct((s, d), BF16)]
    return pl.pallas_call(
        functools.partial(_merge_body, final=final),
        grid=(s // tm,),
        out_specs=out_specs,
        out_shape=out_shape,
        in_specs=[
            pl.BlockSpec((tm, w), lambda i: (i, 0)),
            pl.BlockSpec((tm, w), lambda i: (i, 0)),
            pl.BlockSpec((tm, w), lambda i: (i, 0)),
            *[pl.BlockSpec((GATE_CB, tm, LANES), functools.partial(lambda i, b: (b, i, 0), b=CB_GA // GATE_CB + b))
              for b in range(3 * d // LANES // GATE_CB)],
            pl.BlockSpec((w, d), lambda i: (0, 0), **const),
            pl.BlockSpec((w, d), lambda i: (0, 0), **const),
            pl.BlockSpec((w, d), lambda i: (0, 0), **const),
            pl.BlockSpec((d, d), lambda i: (0, 0), **const),
            row_spec,
            pl.BlockSpec((1, 1, d), lambda i: (layer, 0, 2)),
            pl.BlockSpec((1, d), lambda i: (0, 0)),
            *tail_specs,
        ],
        compiler_params=_params("arbitrary"),
        name="merge",
    )(ga, gb, gc, *([proj] * 6), pa, pb, pc, wo, h, mod, tail_norm_w.reshape(1, d), *tail_args)


def kernel(x, c, w_in, w_proj_a, w_proj_b, w_proj_c, w_out, w_ada, b_ada, norm_w, lam, da_norm_w, ret_norm_w,
           final_norm_w):
    batch, s, d = x.shape
    assert batch == 1 and d == D_MODEL and s % DIL_TILE == 0 and w_in.shape[2] == IN_WIDTH
    depth = w_in.shape[0]
    mod = _adaln(c, w_ada, b_ada)
    h = x[0]
    u = _modnorm(h, norm_w[0], mod, 0)
    for layer in range(depth):
        final = layer == depth - 1
        lam_init = 0.8 - 0.6 * math.exp(-0.3 * layer)
        proj = _inproj(u, layer, w_in)
        ga = _diffattn(proj, lam[layer], da_norm_w[layer], lam_init)
        gb = _retention(proj, ret_norm_w[layer])
        gc = _dilated(proj)
        res = _merge(ga, gb, gc, proj,
                     w_proj_a[layer].astype(BF16), w_proj_b[layer].astype(BF16), w_proj_c[layer].astype(BF16),
                     w_out[layer].astype(BF16), h, mod, layer,
                     final_norm_w if final else norm_w[layer + 1], final)
        if final:
            return res[None]
        h, u = res
```

```python
import functools
import math

import numpy as np
import jax
import jax.numpy as jnp
from jax import lax
from jax.experimental import pallas as pl
from jax.experimental.pallas import tpu as pltpu

F32 = jnp.float32
BF16 = jnp.bfloat16

LANES = 128
VMEM_LIMIT_BYTES = 56 * 1024 * 1024

EPS = 1e-6
LOG2E = math.log2(math.e)

D_MODEL = 2048
DA_HEADS = 8
DA_QK_DIM = 64
DA_V_DIM = 128
RET_HEADS = 8
RET_QK_DIM = 64
RET_V_DIM = 128
RET_CHUNK = 256
ROT_BASE = 10000.0
DIL_DILATIONS = (1, 4, 16)
DIL_RADIUS = 128
DIL_HEADS = 8
DIL_HEAD_DIM = 128
DIL_TILE = 2048
DIL_SPLIT_STRIDE = 4
DIL_UNROLL = 16
MASK_VALUE = -1e30

CB_AQ, CB_AK, CB_AV, CB_AZ = 0, 8, 16, 24
CB_BQ, CB_BK, CB_BV, CB_BZ = 32, 36, 40, 48
CB_CQ, CB_CK, CB_CV, CB_CZ = 56, 80, 104, 128
CB_GA, CB_GB, CB_GC = 136, 152, 168
GATE_CB = 8
IN_WIDTH = 184 * LANES


def _params(*sem, flags=None):
    return pltpu.CompilerParams(dimension_semantics=sem, vmem_limit_bytes=VMEM_LIMIT_BYTES, flags=flags)


def _silu(z):
    return z * jax.nn.sigmoid(z)


def _adaln_body(c_ref, w_ref, b_ref, o_ref):
    cc = c_ref[...]
    o_ref[0] = jnp.sum(_silu(cc) * w_ref[0], axis=0, keepdims=True) + b_ref[0]


def _adaln(c, w_ada, b_ada):
    n_layers, d, e = w_ada.shape
    tn = 768
    return pl.pallas_call(
        _adaln_body,
        grid=(n_layers, e // tn),
        in_specs=[
            pl.BlockSpec((d, 1), lambda l, j: (0, 0)),
            pl.BlockSpec((1, d, tn), lambda l, j: (l, 0, j)),
            pl.BlockSpec((1, 1, tn), lambda l, j: (l, 0, j)),
        ],
        out_specs=pl.BlockSpec((1, 1, tn), lambda l, j: (l, 0, j)),
        out_shape=jax.ShapeDtypeStruct((n_layers, 1, e), F32),
        compiler_params=_params("arbitrary", "arbitrary"),
        name="adaln",
    )(c.reshape(d, 1), w_ada, b_ada.reshape(n_layers, 1, e))


def _modulated_norm(xf, nw, scale, shift):
    y = xf * lax.rsqrt(jnp.mean(xf * xf, axis=-1, keepdims=True) + EPS) * nw
    return (y * (1.0 + scale) + shift).astype(BF16)


def _modnorm_body(x_ref, nw_ref, shift_ref, scale_ref, u_ref):
    u_ref[...] = _modulated_norm(x_ref[...], nw_ref[...], scale_ref[0], shift_ref[0])


def _modnorm(x, norm_w, mod, layer):
    s, d = x.shape
    tm = 512
    return pl.pallas_call(
        _modnorm_body,
        grid=(s // tm,),
        in_specs=[
            pl.BlockSpec((tm, d), lambda i: (i, 0)),
            pl.BlockSpec((1, d), lambda i: (0, 0)),
            pl.BlockSpec((1, 1, d), lambda i: (layer, 0, 0)),
            pl.BlockSpec((1, 1, d), lambda i: (layer, 0, 1)),
        ],
        out_specs=pl.BlockSpec((tm, d), lambda i: (i, 0)),
        out_shape=jax.ShapeDtypeStruct((s, d), BF16),
        compiler_params=_params("arbitrary"),
        name="modnorm",
    )(x, norm_w.reshape(1, d), mod, mod)


def _inproj_body(u_ref, w_ref, o_ref, *, tn, a_scale, c_scale):
    j = pl.program_id(1)
    acc = jnp.dot(u_ref[...], w_ref[...].astype(BF16), preferred_element_type=F32)
    cb = j * (tn // LANES)
    qscale = jnp.where(cb < CB_AK, a_scale, jnp.where((cb >= CB_CQ) & (cb < CB_CK), c_scale, 1.0))
    acc = acc * qscale
    for cblk in range(tn // LANES):
        o_ref[cblk] = acc[:, cblk * LANES:(cblk + 1) * LANES].astype(BF16)


def _inproj(u, layer, w_in):
    s, d = u.shape
    tm, tn = 2048, 1024
    body = functools.partial(
        _inproj_body, tn=tn,
        a_scale=DA_QK_DIM ** -0.5 * LOG2E, c_scale=DIL_HEAD_DIM ** -0.5 * LOG2E)
    return pl.pallas_call(
        body,
        grid=(s // tm, IN_WIDTH // tn),
        in_specs=[
            pl.BlockSpec((tm, d), lambda i, j: (i, 0)),
            pl.BlockSpec((None, d, tn), lambda i, j: (layer, 0, j)),
        ],
        out_specs=pl.BlockSpec((tn // LANES, tm, LANES), lambda i, j: (j, i, 0)),
        out_shape=jax.ShapeDtypeStruct((IN_WIDTH // LANES, s, LANES), BF16),
        compiler_params=_params("arbitrary", "arbitrary"),
        name="inproj",
    )(u, w_in)


DA_HEADS_PER_STEP = 4
DA_SKEW = 4
DA_QCHUNK = 256
DA_ONES_ROWS = 16


def _diffattn_body(lam_ref, nw_ref, q_ref, k_ref, v_ref, z_ref, o_ref, qs_ref, vt_ref, m_ref, acc_ref,
                   *, tq, lam_init):
    i = pl.program_id(1)
    nh = q_ref.shape[0]
    dv = DA_V_DIM

    lane = lax.broadcasted_iota(jnp.int32, (tq, LANES), 1)
    for hh in range(nh):
        q = q_ref[hh]
        zero = jnp.zeros_like(q)
        qs_ref[hh, 0:tq] = jnp.where(lane < DA_QK_DIM, q, zero)
        qs_ref[hh, tq:2 * tq] = jnp.where(lane >= DA_QK_DIM, q, zero)
    m_ref[...] = jnp.full(m_ref.shape, -jnp.inf, F32)
    acc_ref[...] = jnp.zeros(acc_ref.shape, F32)

    def step(kt, diagonal):
        start = pl.multiple_of(kt * tq, tq)
        chains = [(hh, c * DA_QCHUNK) for hh in range(nh) for c in range(2 * tq // DA_QCHUNK)]
        def visible_keys(c0):
            return min(tq, c0 % tq + DA_QCHUNK) if diagonal else tq

        def qk(hh, c0):
            k = k_ref[hh, pl.ds(start, visible_keys(c0)), :]
            st = lax.dot_general(k, qs_ref[hh, c0:c0 + DA_QCHUNK, :], (((1,), (1,)), ((), ())),
                                 preferred_element_type=F32)
            if diagonal:
                key = lax.broadcasted_iota(jnp.int32, st.shape, 0)
                qpos = lax.broadcasted_iota(jnp.int32, st.shape, 1) + (c0 % tq)
                st = jnp.where(key <= qpos, st, -jnp.inf)
            return st.astype(BF16)

        def softmax(hh, c0, st):
            m_prev = m_ref[hh, :, c0:c0 + DA_QCHUNK]
            m_new = jnp.maximum(m_prev, jnp.max(st, axis=0, keepdims=True).astype(F32))
            m_ref[hh, :, c0:c0 + DA_QCHUNK] = m_new
            return jnp.exp2(m_prev - m_new), jnp.exp2(st - m_new.astype(BF16))

        def pv(hh, c0, alpha, pt):
            vt = vt_ref[hh, kt, :, 0:visible_keys(c0)]
            acc_ref[hh, :, c0:c0 + DA_QCHUNK] = (alpha * acc_ref[hh, :, c0:c0 + DA_QCHUNK]
                                                 + jnp.dot(vt, pt, preferred_element_type=F32))

        scores = {}
        for n in range(len(chains) + DA_SKEW):
            if n < len(chains):
                scores[n] = qk(*chains[n])
            if n >= DA_SKEW:
                ch = chains[n - DA_SKEW]
                pv(*ch, *softmax(*ch, scores.pop(n - DA_SKEW)))

    def off_diagonal(kt, carry):
        step(kt, False)
        return carry

    lax.fori_loop(0, i, off_diagonal, 0)
    diag_start = pl.multiple_of(i * tq, tq)
    for hh in range(nh):
        vt_ref[hh, i, 0:dv, :] = v_ref[hh, pl.ds(diag_start, tq), :].astype(F32).T.astype(BF16)
        vt_ref[hh, i, dv:dv + DA_ONES_ROWS, :] = jnp.ones((DA_ONES_ROWS, tq), BF16)
    step(i, True)

    lam = lam_ref[...]
    lam_full = (jnp.exp(jnp.sum(lam[0:1] * lam[1:2], axis=-1, keepdims=True))
                - jnp.exp(jnp.sum(lam[2:3] * lam[3:4], axis=-1, keepdims=True)) + lam_init)
    for hh in range(nh):
        acc = acc_ref[hh]
        inv = 1.0 / acc[dv:dv + 1, :]
        ot = acc[0:dv, 0:tq] * inv[:, 0:tq] - lam_full * (acc[0:dv, tq:2 * tq] * inv[:, tq:2 * tq])
        o = ot.T
        o = o * lax.rsqrt(jnp.mean(o * o, axis=-1, keepdims=True) + EPS) * nw_ref[...] * (1.0 - lam_init)
        o_ref[:, hh * dv:(hh + 1) * dv] = (o * _silu(z_ref[hh].astype(F32))).astype(BF16)


def _diffattn(proj, lam, norm_w, lam_init):
    s = proj.shape[1]
    tq = 512
    nh = DA_HEADS_PER_STEP
    body = functools.partial(_diffattn_body, tq=tq, lam_init=lam_init)
    return pl.pallas_call(
        body,
        grid=(DA_HEADS // nh, s // tq),
        in_specs=[
            pl.BlockSpec((4, DA_QK_DIM), lambda h, i: (0, 0)),
            pl.BlockSpec((1, DA_V_DIM), lambda h, i: (0, 0)),
            pl.BlockSpec((nh, tq, LANES), lambda h, i: (CB_AQ // nh + h, i, 0)),
            pl.BlockSpec((nh, s, LANES), lambda h, i: (CB_AK // nh + h, 0, 0)),
            pl.BlockSpec((nh, s, LANES), lambda h, i: (CB_AV // nh + h, 0, 0)),
            pl.BlockSpec((nh, tq, LANES), lambda h, i: (CB_AZ // nh + h, i, 0)),
        ],
        out_specs=pl.BlockSpec((tq, nh * DA_V_DIM), lambda h, i: (i, h)),
        out_shape=jax.ShapeDtypeStruct((s, DA_HEADS * DA_V_DIM), BF16),
        scratch_shapes=[
            pltpu.VMEM((nh, 2 * tq, LANES), BF16),
            pltpu.VMEM((nh, s // tq, DA_V_DIM + DA_ONES_ROWS, tq), BF16),
            pltpu.VMEM((nh, 1, 2 * tq), F32),
            pltpu.VMEM((nh, DA_V_DIM + DA_ONES_ROWS, 2 * tq), F32),
        ],
        compiler_params=_params("arbitrary", "arbitrary"),
        name="diffattn",
    )(lam, norm_w.reshape(1, DA_V_DIM), proj, proj, proj, proj)


def _retention_tables(s):
    c = RET_CHUNK
    half = RET_QK_DIM // 2
    theta = 1.0 / (ROT_BASE ** jnp.linspace(0.0, 1.0, half, dtype=F32))
    ang = jnp.arange(s, dtype=F32)[:, None] * theta[None, :]
    cos = jnp.tile(jnp.repeat(jnp.cos(ang), 2, axis=-1), (1, 2))
    sign = jnp.tile(jnp.asarray([-1.0, 1.0], F32), RET_QK_DIM)
    sin = jnp.tile(jnp.repeat(jnp.sin(ang), 2, axis=-1), (1, 2)) * sign[None, :]
    log_g = jnp.log1p(-jnp.exp2(-5.0 - jnp.arange(RET_HEADS, dtype=F32)))
    idx = jnp.arange(c, dtype=F32)
    lane_g = jnp.repeat(log_g.reshape(RET_HEADS // 2, 2), RET_QK_DIM, axis=-1)
    qdec = jnp.exp((idx + 1.0)[None, :, None] * lane_g[:, None, :])
    kdec = jnp.exp((c - 1.0 - idx)[None, :, None] * lane_g[:, None, :]) * RET_QK_DIM ** -0.5
    cdec = jnp.broadcast_to(jnp.exp(c * lane_g)[:, :, None], (RET_HEADS // 2, LANES, RET_V_DIM))
    causal = idx[:, None] >= idx[None, :]
    dmask = jnp.where(causal[None], jnp.exp(-c * log_g)[:, None, None], 0.0)
    return cos, sin, qdec, kdec, cdec, dmask.astype(F32)


def _retention_body(cos_ref, sin_ref, qdec_ref, kdec_ref, cdec_ref, dmask_ref, nw_ref,
                    q_ref, k_ref, v_ref, z_ref, o_ref, r_ref):
    @pl.when(pl.program_id(0) == 0)
    def _():
        r_ref[...] = jnp.zeros(r_ref.shape, F32)

    c = q_ref.shape[1]
    cos = cos_ref[...]
    sin = sin_ref[...]
    lane = lax.broadcasted_iota(jnp.int32, (c, LANES), 1)
    even = (lane & 1) == 0
    head0 = lane < RET_QK_DIM

    def rotate(x):
        nxt = pltpu.roll(x, LANES - 1, 1)
        prv = pltpu.roll(x, 1, 1)
        return x * cos + jnp.where(even, nxt, prv) * sin

    row = lax.broadcasted_iota(jnp.int32, r_ref.shape[1:], 0)
    for pp in range(q_ref.shape[0]):
        qd = (rotate(q_ref[pp].astype(F32)) * qdec_ref[pp]).astype(BF16)
        kd = (rotate(k_ref[pp].astype(F32)) * kdec_ref[pp]).astype(BF16)
        zero = jnp.zeros_like(qd)
        state = r_ref[pp]
        state_b = state.astype(BF16)
        updates = []
        for hh in range(2):
            head = 2 * pp + hh
            qz = jnp.where(head0, qd, zero) if hh == 0 else jnp.where(head0, zero, qd)
            sc = lax.dot_general(qz, kd, (((1,), (1,)), ((), ())), preferred_element_type=F32) * dmask_ref[head]
            v = v_ref[head]
            o = (jnp.dot(sc.astype(BF16), v, preferred_element_type=F32)
                 + jnp.dot(qz, state_b, preferred_element_type=F32))
            o = o * lax.rsqrt(jnp.mean(o * o, axis=-1, keepdims=True) + EPS) * nw_ref[...]
            o_ref[:, head * RET_V_DIM:(head + 1) * RET_V_DIM] = (o * _silu(z_ref[head].astype(F32))).astype(BF16)
            updates.append(lax.dot_general(kd, v, (((0,), (0,)), ((), ())), preferred_element_type=F32))
        r_ref[pp] = cdec_ref[pp] * state + jnp.where(row < RET_QK_DIM, updates[0], updates[1])


def _retention(proj, norm_w):
    s = proj.shape[1]
    c = RET_CHUNK
    pairs = RET_HEADS // 2
    cos, sin, qdec, kdec, cdec, dmask = _retention_tables(s)
    return pl.pallas_call(
        _retention_body,
        grid=(s // c,),
        in_specs=[
            pl.BlockSpec((c, LANES), lambda n: (n, 0)),
            pl.BlockSpec((c, LANES), lambda n: (n, 0)),
            pl.BlockSpec((pairs, c, LANES), lambda n: (0, 0, 0)),
            pl.BlockSpec((pairs, c, LANES), lambda n: (0, 0, 0)),
            pl.BlockSpec((pairs, LANES, RET_V_DIM), lambda n: (0, 0, 0)),
            pl.BlockSpec((RET_HEADS, c, c), lambda n: (0, 0, 0)),
            pl.BlockSpec((1, RET_V_DIM), lambda n: (0, 0)),
            pl.BlockSpec((pairs, c, LANES), lambda n: (CB_BQ // pairs, n, 0)),
            pl.BlockSpec((pairs, c, LANES), lambda n: (CB_BK // pairs, n, 0)),
            pl.BlockSpec((RET_HEADS, c, LANES), lambda n: (CB_BV // RET_HEADS, n, 0)),
            pl.BlockSpec((RET_HEADS, c, LANES), lambda n: (CB_BZ // RET_HEADS, n, 0)),
        ],
        out_specs=pl.BlockSpec((c, RET_HEADS * RET_V_DIM), lambda n: (n, 0)),
        out_shape=jax.ShapeDtypeStruct((s, RET_HEADS * RET_V_DIM), BF16),
        scratch_shapes=[pltpu.VMEM((pairs, LANES, RET_V_DIM), F32)],
        compiler_params=_params("arbitrary"),
        name="retention",
    )(cos, sin, qdec, kdec, cdec, dmask, norm_w.reshape(1, RET_V_DIM), proj, proj, proj, proj)


def _dilated_body(*refs):
    (q0, k0, v0, q1, k1, v1, q2, k2, v2, z_ref, o_ref,
     slab, stage_o, stage_l, qp, kp0, vp0, kp1, vp1, kp2, vp2, og0, og1, og2, ls0, ls1, ls2) = refs
    t = pl.program_id(1)
    fence = t < pl.num_programs(1)
    blk = DIL_RADIUS
    groups = (
        (1, q0, k0, v0, kp0, vp0, og0, ls0),
        (4, q1, k1, v1, kp1, vp1, og1, ls1),
        (16, q2, k2, v2, kp2, vp2, og2, ls2),
    )

    qi = lax.broadcasted_iota(jnp.int32, (blk, 2 * blk), 0)
    kj = lax.broadcasted_iota(jnp.int32, (blk, 2 * blk), 1)
    in_window = (kj >= qi) & (kj <= qi + blk)
    bias = jnp.where(in_window, 0.0, MASK_VALUE).astype(F32)
    bias_first = jnp.where(in_window & (kj >= blk), 0.0, MASK_VALUE).astype(F32)

    for d, q_ref, k_ref, v_ref, kp, vp, og, ls in groups:
        m_t = DIL_TILE // d
        nb = m_t // blk

        @pl.when(t == 0)
        def _():
            kp[:, 0:blk, :] = jnp.zeros((d, blk, LANES), BF16)
            vp[:, 0:blk, 0:LANES] = jnp.zeros((d, blk, LANES), BF16)
            vp[:, :, LANES:2 * LANES] = jnp.ones((d, blk + m_t, LANES), BF16)

        if d == 1:
            qp[...] = q_ref[0]
            kp[0, blk:blk + m_t, :] = k_ref[0]
            vp[0, blk:blk + m_t, 0:LANES] = v_ref[0]
        else:
            def to_residue_major(x_ref, store, d=d, m_t=m_t):
                @pl.when(fence)
                def _():
                    slab[...] = x_ref[0].astype(F32)

                @pl.when(fence)
                def _():
                    for r in range(d):
                        store(r, slab[pl.ds(r, m_t, stride=d), :].astype(BF16))

            def store_q(r, rows, m_t=m_t):
                qp[r * m_t:(r + 1) * m_t, :] = rows

            def store_k(r, rows, m_t=m_t, kp=kp):
                kp[r, blk:blk + m_t, :] = rows

            def store_v(r, rows, m_t=m_t, vp=vp):
                vp[r, blk:blk + m_t, 0:LANES] = rows

            to_residue_major(q_ref, store_q)
            to_residue_major(k_ref, store_k)
            to_residue_major(v_ref, store_v)

        def unit(u, carry, d=d, m_t=m_t, nb=nb, kp=kp, vp=vp, og=og, ls=ls):
            r = u // nb
            b = u % nb
            qu = qp[pl.ds(pl.multiple_of(r * m_t + b * blk, blk), blk), :]
            start = pl.multiple_of(b * blk, blk)
            kw = kp[r, pl.ds(start, 2 * blk), :]
            vw = vp[r, pl.ds(start, 2 * blk), :]
            s = lax.dot_general(qu, kw, (((1,), (1,)), ((), ())), preferred_element_type=F32)
            s = s + jnp.where((t == 0) & (b == 0), bias_first, bias)
            m = jnp.max(s, axis=-1, keepdims=True)
            p = jnp.exp2(s - m)
            pv = jnp.dot(p.astype(BF16), vw, preferred_element_type=F32)
            l = pv[:, LANES:2 * LANES]
            o = pv[:, 0:LANES] / l
            lse = m + jnp.log2(l)
            if d == 1:
                row0 = pl.multiple_of(b * blk, blk)
                og[pl.ds(row0, blk), :] = o
                ls[pl.ds(row0, blk), :] = lse
            elif d == DIL_SPLIT_STRIDE ** 2:
                s4 = DIL_SPLIT_STRIDE
                row0 = (r % s4) * (DIL_TILE // s4) + b * (blk * d // s4) + r // s4
                stage_o[pl.ds(row0, blk, stride=d // s4), :] = o
                stage_l[pl.ds(row0, blk, stride=d // s4), :] = lse
            else:
                row0 = b * (blk * d) + r
                og[pl.ds(row0, blk, stride=d), :] = o
                ls[pl.ds(row0, blk, stride=d), :] = lse
            return carry

        @pl.when(fence)
        def _():
            lax.fori_loop(0, d * nb, unit, 0, unroll=DIL_UNROLL)

        if d == DIL_SPLIT_STRIDE ** 2:
            @pl.when(fence)
            def _():
                quarter = DIL_TILE // DIL_SPLIT_STRIDE
                for r0 in range(DIL_SPLIT_STRIDE):
                    og[pl.ds(r0, quarter, stride=DIL_SPLIT_STRIDE), :] = stage_o[r0 * quarter:(r0 + 1) * quarter, :]
                    ls[pl.ds(r0, quarter, stride=DIL_SPLIT_STRIDE), :] = stage_l[r0 * quarter:(r0 + 1) * quarter, :]

        kp[:, 0:blk, :] = kp[:, m_t:m_t + blk, :]
        vp[:, 0:blk, 0:LANES] = vp[:, m_t:m_t + blk, 0:LANES]

    @pl.when(fence)
    def _():
        l0, l1, l2 = ls0[...], ls1[...], ls2[...]
        mx = jnp.maximum(jnp.maximum(l0, l1), l2)
        w0, w1, w2 = jnp.exp2(l0 - mx), jnp.exp2(l1 - mx), jnp.exp2(l2 - mx)
        o = (w0 * og0[...] + w1 * og1[...] + w2 * og2[...]) / (w0 + w1 + w2)
        o_ref[...] = (o * _silu(z_ref[0].astype(F32))).astype(BF16)


def _dilated(proj):
    s = proj.shape[1]
    tile = DIL_TILE
    blk = DIL_RADIUS

    def qkv_spec(base, g):
        return pl.BlockSpec((1, tile, LANES), lambda h, t: (base + g * DIL_HEADS + h, t, 0))

    in_specs = []
    for g in range(len(DIL_DILATIONS)):
        in_specs += [qkv_spec(CB_CQ, g), qkv_spec(CB_CK, g), qkv_spec(CB_CV, g)]
    in_specs.append(pl.BlockSpec((1, tile, LANES), lambda h, t: (CB_CZ + h, t, 0)))
    scratch = [pltpu.VMEM((tile, LANES), F32)] * 3 + [pltpu.VMEM((tile, LANES), BF16)]
    for d in DIL_DILATIONS:
        scratch += [pltpu.VMEM((d, blk + tile // d, LANES), BF16), pltpu.VMEM((d, blk + tile // d, 2 * LANES), BF16)]
    scratch += [pltpu.VMEM((tile, LANES), F32)] * 6
    return pl.pallas_call(
        _dilated_body,
        grid=(DIL_HEADS, s // tile),
        in_specs=in_specs,
        out_specs=pl.BlockSpec((tile, DIL_HEAD_DIM), lambda h, t: (t, h)),
        out_shape=jax.ShapeDtypeStruct((s, DIL_HEADS * DIL_HEAD_DIM), BF16),
        scratch_shapes=scratch,
        compiler_params=_params("arbitrary", "arbitrary"),
        name="dilated",
    )(*([proj] * 10))


def _merge_body(*refs, final):
    (ga_ref, gb_ref, gc_ref, ta0, ta1, tb0, tb1, tc0, tc1, pa_ref, pb_ref, pc_ref, wo_ref,
     h_ref, gate_ref, nw_ref) = refs[:16]

    def branch(g_ref, t_refs, p_ref):
        y = jnp.dot(g_ref[...], p_ref[...], preferred_element_type=F32)
        gate = jnp.concatenate([t[cblk] for t in t_refs for cblk in range(t.shape[0])], axis=-1).astype(F32)
        return jax.nn.sigmoid(gate) * y

    merged = (branch(ga_ref, (ta0, ta1), pa_ref) + branch(gb_ref, (tb0, tb1), pb_ref)
              + branch(gc_ref, (tc0, tc1), pc_ref))
    y = jnp.dot(merged.astype(BF16), wo_ref[...], preferred_element_type=F32)
    out = h_ref[...] + gate_ref[0] * y
    if final:
        (o_ref,) = refs[16:]
        o_ref[...] = out * lax.rsqrt(jnp.mean(out * out, axis=-1, keepdims=True) + EPS) * nw_ref[...]
    else:
        shift_ref, scale_ref, o_ref, u_ref = refs[16:]
        o_ref[...] = out
        u_ref[...] = _modulated_norm(out, nw_ref[...], scale_ref[0], shift_ref[0])


def _merge(ga, gb, gc, proj, pa, pb, pc, wo, h, mod, layer, tail_norm_w, final):
    s, d = h.shape
    w = ga.shape[1]
    tm = 256
    const = dict(pipeline_mode=pl.Buffered(1))
    row_spec = pl.BlockSpec((tm, d), lambda i: (i, 0))
    tail_specs, tail_args = [], []
    out_specs, out_shape = row_spec, jax.ShapeDtypeStruct((s, d), F32)
    if not final:
        tail_specs = [pl.BlockSpec((1, 1, d), lambda i: (layer + 1, 0, 0)),
                      pl.BlockSpec((1, 1, d), lambda i: (layer + 1, 0, 1))]
        tail_args = [mod, mod]
        out_specs, out_shape = [row_spec, row_spec], [out_shape, jax.ShapeDtypeStruct((s, d), BF16)]
    return pl.pallas_call(
        functools.partial(_merge_body, final=final),
        grid=(s // tm,),
        out_specs=out_specs,
        out_shape=out_shape,
        in_specs=[
            pl.BlockSpec((tm, w), lambda i: (i, 0)),
            pl.BlockSpec((tm, w), lambda i: (i, 0)),
            pl.BlockSpec((tm, w), lambda i: (i, 0)),
            *[pl.BlockSpec((GATE_CB, tm, LANES), functools.partial(lambda i, b: (b, i, 0), b=CB_GA // GATE_CB + b))
              for b in range(3 * d // LANES // GATE_CB)],
            pl.BlockSpec((w, d), lambda i: (0, 0), **const),
            pl.BlockSpec((w, d), lambda i: (0, 0), **const),
            pl.BlockSpec((w, d), lambda i: (0, 0), **const),
            pl.BlockSpec((d, d), lambda i: (0, 0), **const),
            row_spec,
            pl.BlockSpec((1, 1, d), lambda i: (layer, 0, 2)),
            pl.BlockSpec((1, d), lambda i: (0, 0)),
            *tail_specs,
        ],
        compiler_params=_params("arbitrary"),
        name="merge",
    )(ga, gb, gc, *([proj] * 6), pa, pb, pc, wo, h, mod, tail_norm_w.reshape(1, d), *tail_args)


def kernel(x, c, w_in, w_proj_a, w_proj_b, w_proj_c, w_out, w_ada, b_ada, norm_w, lam, da_norm_w, ret_norm_w,
           final_norm_w):
    batch, s, d = x.shape
    assert batch == 1 and d == D_MODEL and s % DIL_TILE == 0 and w_in.shape[2] == IN_WIDTH
    depth = w_in.shape[0]
    mod = _adaln(c, w_ada, b_ada)
    h = x[0]
    u = _modnorm(h, norm_w[0], mod, 0)
    for layer in range(depth):
        final = layer == depth - 1
        lam_init = 0.8 - 0.6 * math.exp(-0.3 * layer)
        proj = _inproj(u, layer, w_in)
        ga = _diffattn(proj, lam[layer], da_norm_w[layer], lam_init)
        gb = _retention(proj, ret_norm_w[layer])
        gc = _dilated(proj)
        res = _merge(ga, gb, gc, proj,
                     w_proj_a[layer].astype(BF16), w_proj_b[layer].astype(BF16), w_proj_c[layer].astype(BF16),
                     w_out[layer].astype(BF16), h, mod, layer,
                     final_norm_w if final else norm_w[layer + 1], final)
        if final:
            return res[None]
        h, u = res
```

```python
import functools
import math

import numpy as np
import jax
import jax.numpy as jnp
from jax import lax
from jax.experimental import pallas as pl
from jax.experimental.pallas import tpu as pltpu

F32 = jnp.float32
BF16 = jnp.bfloat16

LANES = 128
VMEM_LIMIT_BYTES = 56 * 1024 * 1024

EPS = 1e-6
LOG2E = math.log2(math.e)

D_MODEL = 2048
DA_HEADS = 8
DA_QK_DIM = 64
DA_V_DIM = 128
RET_HEADS = 8
RET_QK_DIM = 64
RET_V_DIM = 128
RET_CHUNK = 256
ROT_BASE = 10000.0
DIL_DILATIONS = (1, 4, 16)
DIL_RADIUS = 128
DIL_HEADS = 8
DIL_HEAD_DIM = 128
DIL_TILE = 2048
DIL_SPLIT_STRIDE = 4
DIL_UNROLL = 16
MASK_VALUE = -1e30

CB_AQ, CB_AK, CB_AV, CB_AZ = 0, 8, 16, 24
CB_BQ, CB_BK, CB_BV, CB_BZ = 32, 36, 40, 48
CB_CQ, CB_CK, CB_CV, CB_CZ = 56, 80, 104, 128
CB_GA, CB_GB, CB_GC = 136, 152, 168
GATE_CB = 8
IN_WIDTH = 184 * LANES


def _params(*sem, flags=None):
    return pltpu.CompilerParams(dimension_semantics=sem, vmem_limit_bytes=VMEM_LIMIT_BYTES, flags=flags)


def _silu(z):
    return z * jax.nn.sigmoid(z)


def _adaln_body(c_ref, w_ref, b_ref, o_ref):
    cc = c_ref[...]
    o_ref[0] = jnp.sum(_silu(cc) * w_ref[0], axis=0, keepdims=True) + b_ref[0]


def _adaln(c, w_ada, b_ada):
    n_layers, d, e = w_ada.shape
    tn = 768
    return pl.pallas_call(
        _adaln_body,
        grid=(n_layers, e // tn),
        in_specs=[
            pl.BlockSpec((d, 1), lambda l, j: (0, 0)),
            pl.BlockSpec((1, d, tn), lambda l, j: (l, 0, j)),
            pl.BlockSpec((1, 1, tn), lambda l, j: (l, 0, j)),
        ],
        out_specs=pl.BlockSpec((1, 1, tn), lambda l, j: (l, 0, j)),
        out_shape=jax.ShapeDtypeStruct((n_layers, 1, e), F32),
        compiler_params=_params("arbitrary", "arbitrary"),
        name="adaln",
    )(c.reshape(d, 1), w_ada, b_ada.reshape(n_layers, 1, e))


def _modulated_norm(xf, nw, scale, shift):
    y = xf * lax.rsqrt(jnp.mean(xf * xf, axis=-1, keepdims=True) + EPS) * nw
    return (y * (1.0 + scale) + shift).astype(BF16)


def _modnorm_body(x_ref, nw_ref, shift_ref, scale_ref, u_ref):
    u_ref[...] = _modulated_norm(x_ref[...], nw_ref[...], scale_ref[0], shift_ref[0])


def _modnorm(x, norm_w, mod, layer):
    s, d = x.shape
    tm = 512
    return pl.pallas_call(
        _modnorm_body,
        grid=(s // tm,),
        in_specs=[
            pl.BlockSpec((tm, d), lambda i: (i, 0)),
            pl.BlockSpec((1, d), lambda i: (0, 0)),
            pl.BlockSpec((1, 1, d), lambda i: (layer, 0, 0)),
            pl.BlockSpec((1, 1, d), lambda i: (layer, 0, 1)),
        ],
        out_specs=pl.BlockSpec((tm, d), lambda i: (i, 0)),
        out_shape=jax.ShapeDtypeStruct((s, d), BF16),
        compiler_params=_params("arbitrary"),
        name="modnorm",
    )(x, norm_w.reshape(1, d), mod, mod)


def _inproj_body(u_ref, w_ref, o_ref, *, tn, a_scale, c_scale):
    j = pl.program_id(1)
    acc = jnp.dot(u_ref[...], w_ref[...].astype(BF16), preferred_element_type=F32)
    cb = j * (tn // LANES)
    qscale = jnp.where(cb < CB_AK, a_scale, jnp.where((cb >= CB_CQ) & (cb < CB_CK), c_scale, 1.0))
    acc = acc * qscale
    for cblk in range(tn // LANES):
        o_ref[cblk] = acc[:, cblk * LANES:(cblk + 1) * LANES].astype(BF16)


def _inproj(u, layer, w_in):
    s, d = u.shape
    tm, tn = 2048, 1024
    body = functools.partial(
        _inproj_body, tn=tn,
        a_scale=DA_QK_DIM ** -0.5 * LOG2E, c_scale=DIL_HEAD_DIM ** -0.5 * LOG2E)
    return pl.pallas_call(
        body,
        grid=(s // tm, IN_WIDTH // tn),
        in_specs=[
            pl.BlockSpec((tm, d), lambda i, j: (i, 0)),
            pl.BlockSpec((None, d, tn), lambda i, j: (layer, 0, j)),
        ],
        out_specs=pl.BlockSpec((tn // LANES, tm, LANES), lambda i, j: (j, i, 0)),
        out_shape=jax.ShapeDtypeStruct((IN_WIDTH // LANES, s, LANES), BF16),
        compiler_params=_params("arbitrary", "arbitrary"),
        name="inproj",
    )(u, w_in)


DA_HEADS_PER_STEP = 4
DA_SKEW = 4
DA_QCHUNK = 256
DA_ONES_ROWS = 16


def _diffattn_body(lam_ref, nw_ref, q_ref, k_ref, v_ref, z_ref, o_ref, qs_ref, vt_ref, m_ref, acc_ref,
                   *, tq, lam_init):
    i = pl.program_id(1)
    nh = q_ref.shape[0]
    dv = DA_V_DIM

    lane = lax.broadcasted_iota(jnp.int32, (tq, LANES), 1)
    for hh in range(nh):
        q = q_ref[hh]
        zero = jnp.zeros_like(q)
        qs_ref[hh, 0:tq] = jnp.where(lane < DA_QK_DIM, q, zero)
        qs_ref[hh, tq:2 * tq] = jnp.where(lane >= DA_QK_DIM, q, zero)
    m_ref[...] = jnp.full(m_ref.shape, -jnp.inf, F32)
    acc_ref[...] = jnp.zeros(acc_ref.shape, F32)

    def step(kt, diagonal):
        start = pl.multiple_of(kt * tq, tq)
        chains = [(hh, c * DA_QCHUNK) for hh in range(nh) for c in range(2 * tq // DA_QCHUNK)]
        def visible_keys(c0):
            return min(tq, c0 % tq + DA_QCHUNK) if diagonal else tq

        def qk(hh, c0):
            k = k_ref[hh, pl.ds(start, visible_keys(c0)), :]
            st = lax.dot_general(k, qs_ref[hh, c0:c0 + DA_QCHUNK, :], (((1,), (1,)), ((), ())),
                                 preferred_element_type=F32)
            if diagonal:
                key = lax.broadcasted_iota(jnp.int32, st.shape, 0)
                qpos = lax.broadcasted_iota(jnp.int32, st.shape, 1) + (c0 % tq)
                st = jnp.where(key <= qpos, st, -jnp.inf)
            return st.astype(BF16)

        def softmax(hh, c0, st):
            m_prev = m_ref[hh, :, c0:c0 + DA_QCHUNK]
            m_new = jnp.maximum(m_prev, jnp.max(st, axis=0, keepdims=True).astype(F32))
            m_ref[hh, :, c0:c0 + DA_QCHUNK] = m_new
            return jnp.exp2(m_prev - m_new), jnp.exp2(st - m_new.astype(BF16))

        def pv(hh, c0, alpha, pt):
            vt = vt_ref[hh, kt, :, 0:visible_keys(c0)]
            acc_ref[hh, :, c0:c0 + DA_QCHUNK] = (alpha * acc_ref[hh, :, c0:c0 + DA_QCHUNK]
                                                 + jnp.dot(vt, pt, preferred_element_type=F32))

        scores = {}
        for n in range(len(chains) + DA_SKEW):
            if n < len(chains):
                scores[n] = qk(*chains[n])
            if n >= DA_SKEW:
                ch = chains[n - DA_SKEW]
                pv(*ch, *softmax(*ch, scores.pop(n - DA_SKEW)))

    def off_diagonal(kt, carry):
        step(kt, False)
        return carry

    lax.fori_loop(0, i, off_diagonal, 0)
    diag_start = pl.multiple_of(i * tq, tq)
    for hh in range(nh):
        vt_ref[hh, i, 0:dv, :] = v_ref[hh, pl.ds(diag_start, tq), :].astype(F32).T.astype(BF16)
        vt_ref[hh, i, dv:dv + DA_ONES_ROWS, :] = jnp.ones((DA_ONES_ROWS, tq), BF16)
    step(i, True)

    lam = lam_ref[...]
    lam_full = (jnp.exp(jnp.sum(lam[0:1] * lam[1:2], axis=-1, keepdims=True))
                - jnp.exp(jnp.sum(lam[2:3] * lam[3:4], axis=-1, keepdims=True)) + lam_init)
    for hh in range(nh):
        acc = acc_ref[hh]
        inv = 1.0 / acc[dv:dv + 1, :]
        ot = acc[0:dv, 0:tq] * inv[:, 0:tq] - lam_full * (acc[0:dv, tq:2 * tq] * inv[:, tq:2 * tq])
        o = ot.T
        o = o * lax.rsqrt(jnp.mean(o * o, axis=-1, keepdims=True) + EPS) * nw_ref[...] * (1.0 - lam_init)
        o_ref[:, hh * dv:(hh + 1) * dv] = (o * _silu(z_ref[hh].astype(F32))).astype(BF16)


def _diffattn(proj, lam, norm_w, lam_init):
    s = proj.shape[1]
    tq = 512
    nh = DA_HEADS_PER_STEP
    body = functools.partial(_diffattn_body, tq=tq, lam_init=lam_init)
    return pl.pallas_call(
        body,
        grid=(DA_HEADS // nh, s // tq),
        in_specs=[
            pl.BlockSpec((4, DA_QK_DIM), lambda h, i: (0, 0)),
            pl.BlockSpec((1, DA_V_DIM), lambda h, i: (0, 0)),
            pl.BlockSpec((nh, tq, LANES), lambda h, i: (CB_AQ // nh + h, i, 0)),
            pl.BlockSpec((nh, s, LANES), lambda h, i: (CB_AK // nh + h, 0, 0)),
            pl.BlockSpec((nh, s, LANES), lambda h, i: (CB_AV // nh + h, 0, 0)),
            pl.BlockSpec((nh, tq, LANES), lambda h, i: (CB_AZ // nh + h, i, 0)),
        ],
        out_specs=pl.BlockSpec((tq, nh * DA_V_DIM), lambda h, i: (i, h)),
        out_shape=jax.ShapeDtypeStruct((s, DA_HEADS * DA_V_DIM), BF16),
        scratch_shapes=[
            pltpu.VMEM((nh, 2 * tq, LANES), BF16),
            pltpu.VMEM((nh, s // tq, DA_V_DIM + DA_ONES_ROWS, tq), BF16),
            pltpu.VMEM((nh, 1, 2 * tq), F32),
            pltpu.VMEM((nh, DA_V_DIM + DA_ONES_ROWS, 2 * tq), F32),
        ],
        compiler_params=_params("arbitrary", "arbitrary"),
        name="diffattn",
    )(lam, norm_w.reshape(1, DA_V_DIM), proj, proj, proj, proj)


def _retention_tables(s):
    c = RET_CHUNK
    half = RET_QK_DIM // 2
    theta = 1.0 / (ROT_BASE ** jnp.linspace(0.0, 1.0, half, dtype=F32))
    ang = jnp.arange(s, dtype=F32)[:, None] * theta[None, :]
    cos = jnp.tile(jnp.repeat(jnp.cos(ang), 2, axis=-1), (1, 2))
    sign = jnp.tile(jnp.asarray([-1.0, 1.0], F32), RET_QK_DIM)
    sin = jnp.tile(jnp.repeat(jnp.sin(ang), 2, axis=-1), (1, 2)) * sign[None, :]
    log_g = jnp.log1p(-jnp.exp2(-5.0 - jnp.arange(RET_HEADS, dtype=F32)))
    idx = jnp.arange(c, dtype=F32)
    lane_g = jnp.repeat(log_g.reshape(RET_HEADS // 2, 2), RET_QK_DIM, axis=-1)
    qdec = jnp.exp((idx + 1.0)[None, :, None] * lane_g[:, None, :])
    kdec = jnp.exp((c - 1.0 - idx)[None, :, None] * lane_g[:, None, :]) * RET_QK_DIM ** -0.5
    cdec = jnp.broadcast_to(jnp.exp(c * lane_g)[:, :, None], (RET_HEADS // 2, LANES, RET_V_DIM))
    causal = idx[:, None] >= idx[None, :]
    dmask = jnp.where(causal[None], jnp.exp(-c * log_g)[:, None, None], 0.0)
    return cos, sin, qdec, kdec, cdec, dmask.astype(F32)


def _retention_body(cos_ref, sin_ref, qdec_ref, kdec_ref, cdec_ref, dmask_ref, nw_ref,
                    q_ref, k_ref, v_ref, z_ref, o_ref, r_ref):
    @pl.when(pl.program_id(0) == 0)
    def _():
        r_ref[...] = jnp.zeros(r_ref.shape, F32)

    c = q_ref.shape[1]
    cos = cos_ref[...]
    sin = sin_ref[...]
    lane = lax.broadcasted_iota(jnp.int32, (c, LANES), 1)
    even = (lane & 1) == 0
    head0 = lane < RET_QK_DIM

    def rotate(x):
        nxt = pltpu.roll(x, LANES - 1, 1)
        prv = pltpu.roll(x, 1, 1)
        return x * cos + jnp.where(even, nxt, prv) * sin

    row = lax.broadcasted_iota(jnp.int32, r_ref.shape[1:], 0)
    for pp in range(q_ref.shape[0]):
        qd = (rotate(q_ref[pp].astype(F32)) * qdec_ref[pp]).astype(BF16)
        kd = (rotate(k_ref[pp].astype(F32)) * kdec_ref[pp]).astype(BF16)
        zero = jnp.zeros_like(qd)
        state = r_ref[pp]
        state_b = state.astype(BF16)
        updates = []
        for hh in range(2):
            head = 2 * pp + hh
            qz = jnp.where(head0, qd, zero) if hh == 0 else jnp.where(head0, zero, qd)
            sc = lax.dot_general(qz, kd, (((1,), (1,)), ((), ())), preferred_element_type=F32) * dmask_ref[head]
            v = v_ref[head]
            o = (jnp.dot(sc.astype(BF16), v, preferred_element_type=F32)
                 + jnp.dot(qz, state_b, preferred_element_type=F32))
            o = o * lax.rsqrt(jnp.mean(o * o, axis=-1, keepdims=True) + EPS) * nw_ref[...]
            o_ref[:, head * RET_V_DIM:(head + 1) * RET_V_DIM] = (o * _silu(z_ref[head].astype(F32))).astype(BF16)
            updates.append(lax.dot_general(kd, v, (((0,), (0,)), ((), ())), preferred_element_type=F32))
        r_ref[pp] = cdec_ref[pp] * state + jnp.where(row < RET_QK_DIM, updates[0], updates[1])


def _retention(proj, norm_w):
    s = proj.shape[1]
    c = RET_CHUNK
    pairs = RET_HEADS // 2
    cos, sin, qdec, kdec, cdec, dmask = _retention_tables(s)
    return pl.pallas_call(
        _retention_body,
        grid=(s // c,),
        in_specs=[
            pl.BlockSpec((c, LANES), lambda n: (n, 0)),
            pl.BlockSpec((c, LANES), lambda n: (n, 0)),
            pl.BlockSpec((pairs, c, LANES), lambda n: (0, 0, 0)),
            pl.BlockSpec((pairs, c, LANES), lambda n: (0, 0, 0)),
            pl.BlockSpec((pairs, LANES, RET_V_DIM), lambda n: (0, 0, 0)),
            pl.BlockSpec((RET_HEADS, c, c), lambda n: (0, 0, 0)),
            pl.BlockSpec((1, RET_V_DIM), lambda n: (0, 0)),
            pl.BlockSpec((pairs, c, LANES), lambda n: (CB_BQ // pairs, n, 0)),
            pl.BlockSpec((pairs, c, LANES), lambda n: (CB_BK // pairs, n, 0)),
            pl.BlockSpec((RET_HEADS, c, LANES), lambda n: (CB_BV // RET_HEADS, n, 0)),
            pl.BlockSpec((RET_HEADS, c, LANES), lambda n: (CB_BZ // RET_HEADS, n, 0)),
        ],
        out_specs=pl.BlockSpec((c, RET_HEADS * RET_V_DIM), lambda n: (n, 0)),
        out_shape=jax.ShapeDtypeStruct((s, RET_HEADS * RET_V_DIM), BF16),
        scratch_shapes=[pltpu.VMEM((pairs, LANES, RET_V_DIM), F32)],
        compiler_params=_params("arbitrary"),
        name="retention",
    )(cos, sin, qdec, kdec, cdec, dmask, norm_w.reshape(1, RET_V_DIM), proj, proj, proj, proj)


def _dilated_body(*refs):
    (q0, k0, v0, q1, k1, v1, q2, k2, v2, z_ref, o_ref, stage_o, stage_l,
     sq1, sk1, sv1, sq2, sk2, sv2, qp0, qp1, qp2, kp0, vp0, kp1, vp1, kp2, vp2, og0, og1, og2, ls0, ls1, ls2) = refs
    t = pl.program_id(1)
    fence = t < pl.num_programs(1)
    blk = DIL_RADIUS
    groups = (
        (1, q0, k0, v0, None, qp0, kp0, vp0, og0, ls0),
        (4, q1, k1, v1, (sq1, sk1, sv1), qp1, kp1, vp1, og1, ls1),
        (16, q2, k2, v2, (sq2, sk2, sv2), qp2, kp2, vp2, og2, ls2),
    )

    qi = lax.broadcasted_iota(jnp.int32, (blk, 2 * blk), 0)
    kj = lax.broadcasted_iota(jnp.int32, (blk, 2 * blk), 1)
    in_window = (kj >= qi) & (kj <= qi + blk)
    bias = jnp.where(in_window, 0.0, MASK_VALUE).astype(F32)
    bias_first = jnp.where(in_window & (kj >= blk), 0.0, MASK_VALUE).astype(F32)

    @pl.when(t == 0)
    def _():
        for d, _, _, _, _, _, kp, vp, _, _ in groups:
            kp[:, 0:blk, :] = jnp.zeros((d, blk, LANES), BF16)
            vp[:, 0:blk, 0:LANES] = jnp.zeros((d, blk, LANES), BF16)
            vp[:, :, LANES:2 * LANES] = jnp.ones((d, blk + DIL_TILE // d, LANES), BF16)

    for d, q_ref, k_ref, v_ref, slabs, qp, kp, vp, og, ls in groups:
        m_t = DIL_TILE // d
        nb = m_t // blk

        if d == 1:
            qp[...] = q_ref[0]
            kp[0, blk:blk + m_t, :] = k_ref[0]
            vp[0, blk:blk + m_t, 0:LANES] = v_ref[0]
        else:
            def residue_rows(x_ref, slab, d=d, m_t=m_t):
                slab[...] = x_ref[0].astype(F32)
                return [slab[pl.ds(r, m_t, stride=d), :].astype(BF16) for r in range(d)]

            for r, rows in enumerate(residue_rows(q_ref, slabs[0])):
                qp[r * m_t:(r + 1) * m_t, :] = rows
            for r, rows in enumerate(residue_rows(k_ref, slabs[1])):
                kp[r, blk:blk + m_t, :] = rows
            for r, rows in enumerate(residue_rows(v_ref, slabs[2])):
                vp[r, blk:blk + m_t, 0:LANES] = rows

        def unit(u, carry, d=d, m_t=m_t, nb=nb, qp=qp, kp=kp, vp=vp, og=og, ls=ls):
            r = u // nb
            b = u % nb
            qu = qp[pl.ds(pl.multiple_of(r * m_t + b * blk, blk), blk), :]
            start = pl.multiple_of(b * blk, blk)
            kw = kp[r, pl.ds(start, 2 * blk), :]
            vw = vp[r, pl.ds(start, 2 * blk), :]
            s = lax.dot_general(qu, kw, (((1,), (1,)), ((), ())), preferred_element_type=F32)
            s = s + jnp.where((t == 0) & (b == 0), bias_first, bias)
            m = jnp.max(s, axis=-1, keepdims=True)
            p = jnp.exp2(s - m)
            pv = jnp.dot(p.astype(BF16), vw, preferred_element_type=F32)
            l = pv[:, LANES:2 * LANES]
            o = pv[:, 0:LANES] / l
            lse = m + jnp.log2(l)
            if d == 1:
                row0 = pl.multiple_of(b * blk, blk)
                og[pl.ds(row0, blk), :] = o
                ls[pl.ds(row0, blk), :] = lse
            elif d == DIL_SPLIT_STRIDE ** 2:
                s4 = DIL_SPLIT_STRIDE
                row0 = (r % s4) * (DIL_TILE // s4) + b * (blk * d // s4) + r // s4
                stage_o[pl.ds(row0, blk, stride=d // s4), :] = o
                stage_l[pl.ds(row0, blk, stride=d // s4), :] = lse
            else:
                row0 = b * (blk * d) + r
                og[pl.ds(row0, blk, stride=d), :] = o
                ls[pl.ds(row0, blk, stride=d), :] = lse
            return carry

        lax.fori_loop(0, d * nb, unit, 0, unroll=DIL_UNROLL)

    @pl.when(fence)
    def _():
        quarter = DIL_TILE // DIL_SPLIT_STRIDE
        for r0 in range(DIL_SPLIT_STRIDE):
            og2[pl.ds(r0, quarter, stride=DIL_SPLIT_STRIDE), :] = stage_o[r0 * quarter:(r0 + 1) * quarter, :]
            ls2[pl.ds(r0, quarter, stride=DIL_SPLIT_STRIDE), :] = stage_l[r0 * quarter:(r0 + 1) * quarter, :]

    for d, _, _, _, _, _, kp, vp, _, _ in groups:
        m_t = DIL_TILE // d
        kp[:, 0:blk, :] = kp[:, m_t:m_t + blk, :]
        vp[:, 0:blk, 0:LANES] = vp[:, m_t:m_t + blk, 0:LANES]

    @pl.when(fence)
    def _():
        l0, l1, l2 = ls0[...], ls1[...], ls2[...]
        mx = jnp.maximum(jnp.maximum(l0, l1), l2)
        w0, w1, w2 = jnp.exp2(l0 - mx), jnp.exp2(l1 - mx), jnp.exp2(l2 - mx)
        o = (w0 * og0[...] + w1 * og1[...] + w2 * og2[...]) / (w0 + w1 + w2)
        o_ref[...] = (o * _silu(z_ref[0].astype(F32))).astype(BF16)


def _dilated(proj):
    s = proj.shape[1]
    tile = DIL_TILE
    blk = DIL_RADIUS

    def qkv_spec(base, g):
        return pl.BlockSpec((1, tile, LANES), lambda h, t: (base + g * DIL_HEADS + h, t, 0))

    in_specs = []
    for g in range(len(DIL_DILATIONS)):
        in_specs += [qkv_spec(CB_CQ, g), qkv_spec(CB_CK, g), qkv_spec(CB_CV, g)]
    in_specs.append(pl.BlockSpec((1, tile, LANES), lambda h, t: (CB_CZ + h, t, 0)))
    scratch = [pltpu.VMEM((tile, LANES), F32)] * 8 + [pltpu.VMEM((tile, LANES), BF16)] * 3
    for d in DIL_DILATIONS:
        scratch += [pltpu.VMEM((d, blk + tile // d, LANES), BF16), pltpu.VMEM((d, blk + tile // d, 2 * LANES), BF16)]
    scratch += [pltpu.VMEM((tile, LANES), F32)] * 6
    return pl.pallas_call(
        _dilated_body,
        grid=(DIL_HEADS, s // tile),
        in_specs=in_specs,
        out_specs=pl.BlockSpec((tile, DIL_HEAD_DIM), lambda h, t: (t, h)),
        out_shape=jax.ShapeDtypeStruct((s, DIL_HEADS * DIL_HEAD_DIM), BF16),
        scratch_shapes=scratch,
        compiler_params=_params("arbitrary", "arbitrary"),
        name="dilated",
    )(*([proj] * 10))


def _merge_body(*refs, final):
    (ga_ref, gb_ref, gc_ref, ta0, ta1, tb0, tb1, tc0, tc1, pa_ref, pb_ref, pc_ref, wo_ref,
     h_ref, gate_ref, nw_ref) = refs[:16]

    def branch(g_ref, t_refs, p_ref):
        y = jnp.dot(g_ref[...], p_ref[...], preferred_element_type=F32)
        gate = jnp.concatenate([t[cblk] for t in t_refs for cblk in range(t.shape[0])], axis=-1).astype(F32)
        return jax.nn.sigmoid(gate) * y

    merged = (branch(ga_ref, (ta0, ta1), pa_ref) + branch(gb_ref, (tb0, tb1), pb_ref)
              + branch(gc_ref, (tc0, tc1), pc_ref))
    y = jnp.dot(merged.astype(BF16), wo_ref[...], preferred_element_type=F32)
    out = h_ref[...] + gate_ref[0] * y
    if final:
        (o_ref,) = refs[16:]
        o_ref[...] = out * lax.rsqrt(jnp.mean(out * out, axis=-1, keepdims=True) + EPS) * nw_ref[...]
    else:
        shift_ref, scale_ref, o_ref, u_ref = refs[16:]
        o_ref[...] = out
        u_ref[...] = _modulated_norm(out, nw_ref[...], scale_ref[0], shift_ref[0])


def _merge(ga, gb, gc, proj, pa, pb, pc, wo, h, mod, layer, tail_norm_w, final):
    s, d = h.shape
    w = ga.shape[1]
    tm = 256
    const = dict(pipeline_mode=pl.Buffered(1))
    row_spec = pl.BlockSpec((tm, d), lambda i: (i, 0))
    tail_specs, tail_args = [], []
    out_specs, out_shape = row_spec, jax.ShapeDtypeStruct((s, d), F32)
    if not final:
        tail_specs = [pl.BlockSpec((1, 1, d), lambda i: (layer + 1, 0, 0)),
                      pl.BlockSpec((1, 1, d), lambda i: (layer + 1, 0, 1))]
        tail_args = [mod, mod]
        out_specs, out_shape = [row_spec, row_spec], [out_shape, jax.ShapeDtypeStruct((s, d), BF16)]
    return pl.pallas_call(
        functools.partial(_merge_body, final=final),
        grid=(s // tm,),
        out_specs=out_specs,
        out_shape=out_shape,
        in_specs=[
            pl.BlockSpec((tm, w), lambda i: (i, 0)),
            pl.BlockSpec((tm, w), lambda i: (i, 0)),
            pl.BlockSpec((tm, w), lambda i: (i, 0)),
            *[pl.BlockSpec((GATE_CB, tm, LANES), functools.partial(lambda i, b: (b, i, 0), b=CB_GA // GATE_CB + b))
              for b in range(3 * d // LANES // GATE_CB)],
            pl.BlockSpec((w, d), lambda i: (0, 0), **const),
            pl.BlockSpec((w, d), lambda i: (0, 0), **const),
            pl.BlockSpec((w, d), lambda i: (0, 0), **const),
            pl.BlockSpec((d, d), lambda i: (0, 0), **const),
            row_spec,
            pl.BlockSpec((1, 1, d), lambda i: (layer, 0, 2)),
            pl.BlockSpec((1, d), lambda i: (0, 0)),
            *tail_specs,
        ],
        compiler_params=_params("arbitrary"),
        name="merge",
    )(ga, gb, gc, *([proj] * 6), pa, pb, pc, wo, h, mod, tail_norm_w.reshape(1, d), *tail_args)


def kernel(x, c, w_in, w_proj_a, w_proj_b, w_proj_c, w_out, w_ada, b_ada, norm_w, lam, da_norm_w, ret_norm_w,
           final_norm_w):
    batch, s, d = x.shape
    assert batch == 1 and d == D_MODEL and s % DIL_TILE == 0 and w_in.shape[2] == IN_WIDTH
    depth = w_in.shape[0]
    mod = _adaln(c, w_ada, b_ada)
    h = x[0]
    u = _modnorm(h, norm_w[0], mod, 0)
    for layer in range(depth):
        final = layer == depth - 1
        lam_init = 0.8 - 0.6 * math.exp(-0.3 * layer)
        proj = _inproj(u, layer, w_in)
        ga = _diffattn(proj, lam[layer], da_norm_w[layer], lam_init)
        gb = _retention(proj, ret_norm_w[layer])
        gc = _dilated(proj)
        res = _merge(ga, gb, gc, proj,
                     w_proj_a[layer].astype(BF16), w_proj_b[layer].astype(BF16), w_proj_c[layer].astype(BF16),
                     w_out[layer].astype(BF16), h, mod, layer,
                     final_norm_w if final else norm_w[layer + 1], final)
        if final:
            return res[None]
        h, u = res
```

```python
import functools
import math

import numpy as np
import jax
import jax.numpy as jnp
from jax import lax
from jax.experimental import pallas as pl
from jax.experimental.pallas import tpu as pltpu

F32 = jnp.float32
BF16 = jnp.bfloat16

LANES = 128
VMEM_LIMIT_BYTES = 56 * 1024 * 1024

EPS = 1e-6
LOG2E = math.log2(math.e)

D_MODEL = 2048
DA_HEADS = 8
DA_QK_DIM = 64
DA_V_DIM = 128
RET_HEADS = 8
RET_QK_DIM = 64
RET_V_DIM = 128
RET_CHUNK = 256
ROT_BASE = 10000.0
DIL_DILATIONS = (1, 4, 16)
DIL_RADIUS = 128
DIL_HEADS = 8
DIL_HEAD_DIM = 128
DIL_TILE = 2048
DIL_SPLIT_STRIDE = 4
DIL_UNROLL = 16
MASK_VALUE = -1e30

CB_AQ, CB_AK, CB_AV, CB_AZ = 0, 8, 16, 24
CB_BQ, CB_BK, CB_BV, CB_BZ = 32, 36, 40, 48
CB_CQ, CB_CK, CB_CV, CB_CZ = 56, 80, 104, 128
CB_GA, CB_GB, CB_GC = 136, 152, 168
GATE_CB = 8
IN_WIDTH = 184 * LANES


def _params(*sem, flags=None):
    return pltpu.CompilerParams(dimension_semantics=sem, vmem_limit_bytes=VMEM_LIMIT_BYTES, flags=flags)


def _silu(z):
    return z * jax.nn.sigmoid(z)


def _adaln_body(c_ref, w_ref, b_ref, o_ref):
    cc = c_ref[...]
    o_ref[0] = jnp.sum(_silu(cc) * w_ref[0], axis=0, keepdims=True) + b_ref[0]


def _adaln(c, w_ada, b_ada):
    n_layers, d, e = w_ada.shape
    tn = 768
    return pl.pallas_call(
        _adaln_body,
        grid=(n_layers, e // tn),
        in_specs=[
            pl.BlockSpec((d, 1), lambda l, j: (0, 0)),
            pl.BlockSpec((1, d, tn), lambda l, j: (l, 0, j)),
            pl.BlockSpec((1, 1, tn), lambda l, j: (l, 0, j)),
        ],
        out_specs=pl.BlockSpec((1, 1, tn), lambda l, j: (l, 0, j)),
        out_shape=jax.ShapeDtypeStruct((n_layers, 1, e), F32),
        compiler_params=_params("arbitrary", "arbitrary"),
        name="adaln",
    )(c.reshape(d, 1), w_ada, b_ada.reshape(n_layers, 1, e))


def _modulated_norm(xf, nw, scale, shift):
    y = xf * lax.rsqrt(jnp.mean(xf * xf, axis=-1, keepdims=True) + EPS) * nw
    return (y * (1.0 + scale) + shift).astype(BF16)


def _modnorm_body(x_ref, nw_ref, shift_ref, scale_ref, u_ref):
    u_ref[...] = _modulated_norm(x_ref[...], nw_ref[...], scale_ref[0], shift_ref[0])


def _modnorm(x, norm_w, mod, layer):
    s, d = x.shape
    tm = 512
    return pl.pallas_call(
        _modnorm_body,
        grid=(s // tm,),
        in_specs=[
            pl.BlockSpec((tm, d), lambda i: (i, 0)),
            pl.BlockSpec((1, d), lambda i: (0, 0)),
            pl.BlockSpec((1, 1, d), lambda i: (layer, 0, 0)),
            pl.BlockSpec((1, 1, d), lambda i: (layer, 0, 1)),
        ],
        out_specs=pl.BlockSpec((tm, d), lambda i: (i, 0)),
        out_shape=jax.ShapeDtypeStruct((s, d), BF16),
        compiler_params=_params("arbitrary"),
        name="modnorm",
    )(x, norm_w.reshape(1, d), mod, mod)


def _inproj_body(u_ref, w_ref, o_ref, *, tn, a_scale, c_scale):
    j = pl.program_id(1)
    acc = jnp.dot(u_ref[...], w_ref[...].astype(BF16), preferred_element_type=F32)
    cb = j * (tn // LANES)
    qscale = jnp.where(cb < CB_AK, a_scale, jnp.where((cb >= CB_CQ) & (cb < CB_CK), c_scale, 1.0))
    acc = acc * qscale
    for cblk in range(tn // LANES):
        o_ref[cblk] = acc[:, cblk * LANES:(cblk + 1) * LANES].astype(BF16)


def _inproj(u, layer, w_in):
    s, d = u.shape
    tm, tn = 2048, 1024
    body = functools.partial(
        _inproj_body, tn=tn,
        a_scale=DA_QK_DIM ** -0.5 * LOG2E, c_scale=DIL_HEAD_DIM ** -0.5 * LOG2E)
    return pl.pallas_call(
        body,
        grid=(s // tm, IN_WIDTH // tn),
        in_specs=[
            pl.BlockSpec((tm, d), lambda i, j: (i, 0)),
            pl.BlockSpec((None, d, tn), lambda i, j: (layer, 0, j)),
        ],
        out_specs=pl.BlockSpec((tn // LANES, tm, LANES), lambda i, j: (j, i, 0)),
        out_shape=jax.ShapeDtypeStruct((IN_WIDTH // LANES, s, LANES), BF16),
        compiler_params=_params("arbitrary", "arbitrary"),
        name="inproj",
    )(u, w_in)


DA_HEADS_PER_STEP = 4
DA_SKEW = 4
DA_QCHUNK = 256
DA_ONES_ROWS = 16


def _diffattn_body(lam_ref, nw_ref, q_ref, k_ref, v_ref, z_ref, o_ref, qs_ref, vt_ref, m_ref, acc_ref,
                   *, tq, lam_init):
    i = pl.program_id(1)
    nh = q_ref.shape[0]
    dv = DA_V_DIM

    lane = lax.broadcasted_iota(jnp.int32, (tq, LANES), 1)
    for hh in range(nh):
        q = q_ref[hh]
        zero = jnp.zeros_like(q)
        qs_ref[hh, 0:tq] = jnp.where(lane < DA_QK_DIM, q, zero)
        qs_ref[hh, tq:2 * tq] = jnp.where(lane >= DA_QK_DIM, q, zero)
    m_ref[...] = jnp.full(m_ref.shape, -jnp.inf, F32)
    acc_ref[...] = jnp.zeros(acc_ref.shape, F32)

    def step(kt, diagonal):
        start = pl.multiple_of(kt * tq, tq)
        chains = [(hh, c * DA_QCHUNK) for hh in range(nh) for c in range(2 * tq // DA_QCHUNK)]
        def visible_keys(c0):
            return min(tq, c0 % tq + DA_QCHUNK) if diagonal else tq

        def qk(hh, c0):
            k = k_ref[hh, pl.ds(start, visible_keys(c0)), :]
            st = lax.dot_general(k, qs_ref[hh, c0:c0 + DA_QCHUNK, :], (((1,), (1,)), ((), ())),
                                 preferred_element_type=F32)
            if diagonal:
                key = lax.broadcasted_iota(jnp.int32, st.shape, 0)
                qpos = lax.broadcasted_iota(jnp.int32, st.shape, 1) + (c0 % tq)
                st = jnp.where(key <= qpos, st, -jnp.inf)
            return st.astype(BF16)

        def softmax(hh, c0, st):
            m_prev = m_ref[hh, :, c0:c0 + DA_QCHUNK]
            m_new = jnp.maximum(m_prev, jnp.max(st, axis=0, keepdims=True).astype(F32))
            m_ref[hh, :, c0:c0 + DA_QCHUNK] = m_new
            return jnp.exp2(m_prev - m_new), jnp.exp2(st - m_new.astype(BF16))

        def pv(hh, c0, alpha, pt):
            vt = vt_ref[hh, kt, :, 0:visible_keys(c0)]
            acc_ref[hh, :, c0:c0 + DA_QCHUNK] = (alpha * acc_ref[hh, :, c0:c0 + DA_QCHUNK]
                                                 + jnp.dot(vt, pt, preferred_element_type=F32))

        scores = {}
        for n in range(len(chains) + DA_SKEW):
            if n < len(chains):
                scores[n] = qk(*chains[n])
            if n >= DA_SKEW:
                ch = chains[n - DA_SKEW]
                pv(*ch, *softmax(*ch, scores.pop(n - DA_SKEW)))

    def off_diagonal(kt, carry):
        step(kt, False)
        return carry

    lax.fori_loop(0, i, off_diagonal, 0)
    diag_start = pl.multiple_of(i * tq, tq)
    for hh in range(nh):
        vt_ref[hh, i, 0:dv, :] = v_ref[hh, pl.ds(diag_start, tq), :].astype(F32).T.astype(BF16)
        vt_ref[hh, i, dv:dv + DA_ONES_ROWS, :] = jnp.ones((DA_ONES_ROWS, tq), BF16)
    step(i, True)

    lam = lam_ref[...]
    lam_full = (jnp.exp(jnp.sum(lam[0:1] * lam[1:2], axis=-1, keepdims=True))
                - jnp.exp(jnp.sum(lam[2:3] * lam[3:4], axis=-1, keepdims=True)) + lam_init)
    for hh in range(nh):
        acc = acc_ref[hh]
        inv = 1.0 / acc[dv:dv + 1, :]
        ot = acc[0:dv, 0:tq] * inv[:, 0:tq] - lam_full * (acc[0:dv, tq:2 * tq] * inv[:, tq:2 * tq])
        o = ot.T
        o = o * lax.rsqrt(jnp.mean(o * o, axis=-1, keepdims=True) + EPS) * nw_ref[...] * (1.0 - lam_init)
        o_ref[:, hh * dv:(hh + 1) * dv] = (o * _silu(z_ref[hh].astype(F32))).astype(BF16)


def _diffattn(proj, lam, norm_w, lam_init):
    s = proj.shape[1]
    tq = 512
    nh = DA_HEADS_PER_STEP
    body = functools.partial(_diffattn_body, tq=tq, lam_init=lam_init)
    return pl.pallas_call(
        body,
        grid=(DA_HEADS // nh, s // tq),
        in_specs=[
            pl.BlockSpec((4, DA_QK_DIM), lambda h, i: (0, 0)),
            pl.BlockSpec((1, DA_V_DIM), lambda h, i: (0, 0)),
            pl.BlockSpec((nh, tq, LANES), lambda h, i: (CB_AQ // nh + h, i, 0)),
            pl.BlockSpec((nh, s, LANES), lambda h, i: (CB_AK // nh + h, 0, 0)),
            pl.BlockSpec((nh, s, LANES), lambda h, i: (CB_AV // nh + h, 0, 0)),
            pl.BlockSpec((nh, tq, LANES), lambda h, i: (CB_AZ // nh + h, i, 0)),
        ],
        out_specs=pl.BlockSpec((tq, nh * DA_V_DIM), lambda h, i: (i, h)),
        out_shape=jax.ShapeDtypeStruct((s, DA_HEADS * DA_V_DIM), BF16),
        scratch_shapes=[
            pltpu.VMEM((nh, 2 * tq, LANES), BF16),
            pltpu.VMEM((nh, s // tq, DA_V_DIM + DA_ONES_ROWS, tq), BF16),
            pltpu.VMEM((nh, 1, 2 * tq), F32),
            pltpu.VMEM((nh, DA_V_DIM + DA_ONES_ROWS, 2 * tq), F32),
        ],
        compiler_params=_params("arbitrary", "arbitrary"),
        name="diffattn",
    )(lam, norm_w.reshape(1, DA_V_DIM), proj, proj, proj, proj)


def _retention_tables(s):
    c = RET_CHUNK
    half = RET_QK_DIM // 2
    theta = 1.0 / (ROT_BASE ** jnp.linspace(0.0, 1.0, half, dtype=F32))
    ang = jnp.arange(s, dtype=F32)[:, None] * theta[None, :]
    cos = jnp.tile(jnp.repeat(jnp.cos(ang), 2, axis=-1), (1, 2))
    sign = jnp.tile(jnp.asarray([-1.0, 1.0], F32), RET_QK_DIM)
    sin = jnp.tile(jnp.repeat(jnp.sin(ang), 2, axis=-1), (1, 2)) * sign[None, :]
    log_g = jnp.log1p(-jnp.exp2(-5.0 - jnp.arange(RET_HEADS, dtype=F32)))
    idx = jnp.arange(c, dtype=F32)
    lane_g = jnp.repeat(log_g.reshape(RET_HEADS // 2, 2), RET_QK_DIM, axis=-1)
    qdec = jnp.exp((idx + 1.0)[None, :, None] * lane_g[:, None, :])
    kdec = jnp.exp((c - 1.0 - idx)[None, :, None] * lane_g[:, None, :]) * RET_QK_DIM ** -0.5
    cdec = jnp.broadcast_to(jnp.exp(c * lane_g)[:, :, None], (RET_HEADS // 2, LANES, RET_V_DIM))
    causal = idx[:, None] >= idx[None, :]
    dmask = jnp.where(causal[None], jnp.exp(-c * log_g)[:, None, None], 0.0)
    return cos, sin, qdec, kdec, cdec, dmask.astype(F32)


def _retention_body(cos_ref, sin_ref, qdec_ref, kdec_ref, cdec_ref, dmask_ref, nw_ref,
                    q_ref, k_ref, v_ref, z_ref, o_ref, r_ref):
    @pl.when(pl.program_id(0) == 0)
    def _():
        r_ref[...] = jnp.zeros(r_ref.shape, F32)

    c = q_ref.shape[1]
    cos = cos_ref[...]
    sin = sin_ref[...]
    lane = lax.broadcasted_iota(jnp.int32, (c, LANES), 1)
    even = (lane & 1) == 0
    head0 = lane < RET_QK_DIM

    def rotate(x):
        nxt = pltpu.roll(x, LANES - 1, 1)
        prv = pltpu.roll(x, 1, 1)
        return x * cos + jnp.where(even, nxt, prv) * sin

    row = lax.broadcasted_iota(jnp.int32, r_ref.shape[1:], 0)
    for pp in range(q_ref.shape[0]):
        qd = (rotate(q_ref[pp].astype(F32)) * qdec_ref[pp]).astype(BF16)
        kd = (rotate(k_ref[pp].astype(F32)) * kdec_ref[pp]).astype(BF16)
        zero = jnp.zeros_like(qd)
        state = r_ref[pp]
        state_b = state.astype(BF16)
        updates = []
        for hh in range(2):
            head = 2 * pp + hh
            qz = jnp.where(head0, qd, zero) if hh == 0 else jnp.where(head0, zero, qd)
            sc = lax.dot_general(qz, kd, (((1,), (1,)), ((), ())), preferred_element_type=F32) * dmask_ref[head]
            v = v_ref[head]
            o = (jnp.dot(sc.astype(BF16), v, preferred_element_type=F32)
                 + jnp.dot(qz, state_b, preferred_element_type=F32))
            o = o * lax.rsqrt(jnp.mean(o * o, axis=-1, keepdims=True) + EPS) * nw_ref[...]
            o_ref[:, head * RET_V_DIM:(head + 1) * RET_V_DIM] = (o * _silu(z_ref[head].astype(F32))).astype(BF16)
            updates.append(lax.dot_general(kd, v, (((0,), (0,)), ((), ())), preferred_element_type=F32))
        r_ref[pp] = cdec_ref[pp] * state + jnp.where(row < RET_QK_DIM, updates[0], updates[1])


def _retention(proj, norm_w):
    s = proj.shape[1]
    c = RET_CHUNK
    pairs = RET_HEADS // 2
    cos, sin, qdec, kdec, cdec, dmask = _retention_tables(s)
    return pl.pallas_call(
        _retention_body,
        grid=(s // c,),
        in_specs=[
            pl.BlockSpec((c, LANES), lambda n: (n, 0)),
            pl.BlockSpec((c, LANES), lambda n: (n, 0)),
            pl.BlockSpec((pairs, c, LANES), lambda n: (0, 0, 0)),
            pl.BlockSpec((pairs, c, LANES), lambda n: (0, 0, 0)),
            pl.BlockSpec((pairs, LANES, RET_V_DIM), lambda n: (0, 0, 0)),
            pl.BlockSpec((RET_HEADS, c, c), lambda n: (0, 0, 0)),
            pl.BlockSpec((1, RET_V_DIM), lambda n: (0, 0)),
            pl.BlockSpec((pairs, c, LANES), lambda n: (CB_BQ // pairs, n, 0)),
            pl.BlockSpec((pairs, c, LANES), lambda n: (CB_BK // pairs, n, 0)),
            pl.BlockSpec((RET_HEADS, c, LANES), lambda n: (CB_BV // RET_HEADS, n, 0)),
            pl.BlockSpec((RET_HEADS, c, LANES), lambda n: (CB_BZ // RET_HEADS, n, 0)),
        ],
        out_specs=pl.BlockSpec((c, RET_HEADS * RET_V_DIM), lambda n: (n, 0)),
        out_shape=jax.ShapeDtypeStruct((s, RET_HEADS * RET_V_DIM), BF16),
        scratch_shapes=[pltpu.VMEM((pairs, LANES, RET_V_DIM), F32)],
        compiler_params=_params("arbitrary"),
        name="retention",
    )(cos, sin, qdec, kdec, cdec, dmask, norm_w.reshape(1, RET_V_DIM), proj, proj, proj, proj)


def _dilated_body(*refs):
    (q0, k0, v0, q1, k1, v1, q2, k2, v2, z_ref, o_ref, stage_o, stage_l,
     sq1, sk1, sv1, sq2, sk2, sv2, qp0, qp1, qp2, kp0, vp0, kp1, vp1, kp2, vp2, og0, og1, og2, ls0, ls1, ls2) = refs
    t = pl.program_id(1)
    fence = t < pl.num_programs(1)
    blk = DIL_RADIUS
    groups = (
        (1, q0, k0, v0, None, qp0, kp0, vp0, og0, ls0),
        (4, q1, k1, v1, (sq1, sk1, sv1), qp1, kp1, vp1, og1, ls1),
        (16, q2, k2, v2, (sq2, sk2, sv2), qp2, kp2, vp2, og2, ls2),
    )

    qi = lax.broadcasted_iota(jnp.int32, (blk, 2 * blk), 0)
    kj = lax.broadcasted_iota(jnp.int32, (blk, 2 * blk), 1)
    in_window = (kj >= qi) & (kj <= qi + blk)
    bias = jnp.where(in_window, 0.0, MASK_VALUE).astype(BF16)
    bias_first = jnp.where(in_window & (kj >= blk), 0.0, MASK_VALUE).astype(BF16)

    @pl.when(t == 0)
    def _():
        for d, _, _, _, _, _, kp, vp, _, _ in groups:
            kp[:, 0:blk, :] = jnp.zeros((d, blk, LANES), BF16)
            vp[:, 0:blk, 0:LANES] = jnp.zeros((d, blk, LANES), BF16)
            vp[:, :, LANES:2 * LANES] = jnp.ones((d, blk + DIL_TILE // d, LANES), BF16)

    for d, q_ref, k_ref, v_ref, slabs, qp, kp, vp, og, ls in groups:
        m_t = DIL_TILE // d
        nb = m_t // blk

        if d == 1:
            qp[...] = q_ref[0]
            kp[0, blk:blk + m_t, :] = k_ref[0]
            vp[0, blk:blk + m_t, 0:LANES] = v_ref[0]
        else:
            def residue_rows(x_ref, slab, d=d, m_t=m_t):
                slab[...] = x_ref[0].astype(F32)
                return [slab[pl.ds(r, m_t, stride=d), :].astype(BF16) for r in range(d)]

            for r, rows in enumerate(residue_rows(q_ref, slabs[0])):
                qp[r * m_t:(r + 1) * m_t, :] = rows
            for r, rows in enumerate(residue_rows(k_ref, slabs[1])):
                kp[r, blk:blk + m_t, :] = rows
            for r, rows in enumerate(residue_rows(v_ref, slabs[2])):
                vp[r, blk:blk + m_t, 0:LANES] = rows

        def unit(u, carry, d=d, m_t=m_t, nb=nb, qp=qp, kp=kp, vp=vp, og=og, ls=ls):
            r = u // nb
            b = u % nb
            qu = qp[pl.ds(pl.multiple_of(r * m_t + b * blk, blk), blk), :]
            start = pl.multiple_of(b * blk, blk)
            kw = kp[r, pl.ds(start, 2 * blk), :]
            vw = vp[r, pl.ds(start, 2 * blk), :]
            s = lax.dot_general(qu, kw, (((1,), (1,)), ((), ())), preferred_element_type=F32).astype(BF16)
            s = s + jnp.where((t == 0) & (b == 0), bias_first, bias)
            m = jnp.max(s, axis=-1, keepdims=True)
            p = jnp.exp2(s - m)
            pv = jnp.dot(p, vw, preferred_element_type=F32)
            l = pv[:, LANES:2 * LANES]
            o = pv[:, 0:LANES] / l
            lse = m.astype(F32) + jnp.log2(l)
            if d == 1:
                row0 = pl.multiple_of(b * blk, blk)
                og[pl.ds(row0, blk), :] = o
                ls[pl.ds(row0, blk), :] = lse
            elif d == DIL_SPLIT_STRIDE ** 2:
                s4 = DIL_SPLIT_STRIDE
                row0 = (r % s4) * (DIL_TILE // s4) + b * (blk * d // s4) + r // s4
                stage_o[pl.ds(row0, blk, stride=d // s4), :] = o
                stage_l[pl.ds(row0, blk, stride=d // s4), :] = lse
            else:
                row0 = b * (blk * d) + r
                og[pl.ds(row0, blk, stride=d), :] = o
                ls[pl.ds(row0, blk, stride=d), :] = lse
            return carry

        lax.fori_loop(0, d * nb, unit, 0, unroll=DIL_UNROLL)

    @pl.when(fence)
    def _():
        quarter = DIL_TILE // DIL_SPLIT_STRIDE
        for r0 in range(DIL_SPLIT_STRIDE):
            og2[pl.ds(r0, quarter, stride=DIL_SPLIT_STRIDE), :] = stage_o[r0 * quarter:(r0 + 1) * quarter, :]
            ls2[pl.ds(r0, quarter, stride=DIL_SPLIT_STRIDE), :] = stage_l[r0 * quarter:(r0 + 1) * quarter, :]

    for d, _, _, _, _, _, kp, vp, _, _ in groups:
        m_t = DIL_TILE // d
        kp[:, 0:blk, :] = kp[:, m_t:m_t + blk, :]
        vp[:, 0:blk, 0:LANES] = vp[:, m_t:m_t + blk, 0:LANES]

    @pl.when(fence)
    def _():
        l0, l1, l2 = ls0[...], ls1[...], ls2[...]
        mx = jnp.maximum(jnp.maximum(l0, l1), l2)
        w0, w1, w2 = jnp.exp2(l0 - mx), jnp.exp2(l1 - mx), jnp.exp2(l2 - mx)
        o = (w0 * og0[...] + w1 * og1[...] + w2 * og2[...]) / (w0 + w1 + w2)
        o_ref[...] = (o * _silu(z_ref[0].astype(F32))).astype(BF16)


def _dilated(proj):
    s = proj.shape[1]
    tile = DIL_TILE
    blk = DIL_RADIUS

    def qkv_spec(base, g):
        return pl.BlockSpec((1, tile, LANES), lambda h, t: (base + g * DIL_HEADS + h, t, 0))

    in_specs = []
    for g in range(len(DIL_DILATIONS)):
        in_specs += [qkv_spec(CB_CQ, g), qkv_spec(CB_CK, g), qkv_spec(CB_CV, g)]
    in_specs.append(pl.BlockSpec((1, tile, LANES), lambda h, t: (CB_CZ + h, t, 0)))
    scratch = [pltpu.VMEM((tile, LANES), F32)] * 8 + [pltpu.VMEM((tile, LANES), BF16)] * 3
    for d in DIL_DILATIONS:
        scratch += [pltpu.VMEM((d, blk + tile // d, LANES), BF16), pltpu.VMEM((d, blk + tile // d, 2 * LANES), BF16)]
    scratch += [pltpu.VMEM((tile, LANES), F32)] * 6
    return pl.pallas_call(
        _dilated_body,
        grid=(DIL_HEADS, s // tile),
        in_specs=in_specs,
        out_specs=pl.BlockSpec((tile, DIL_HEAD_DIM), lambda h, t: (t, h)),
        out_shape=jax.ShapeDtypeStruct((s, DIL_HEADS * DIL_HEAD_DIM), BF16),
        scratch_shapes=scratch,
        compiler_params=_params("arbitrary", "arbitrary"),
        name="dilated",
    )(*([proj] * 10))


def _merge_body(*refs, final):
    (ga_ref, gb_ref, gc_ref, ta0, ta1, tb0, tb1, tc0, tc1, pa_ref, pb_ref, pc_ref, wo_ref,
     h_ref, gate_ref, nw_ref) = refs[:16]

    def branch(g_ref, t_refs, p_ref):
        y = jnp.dot(g_ref[...], p_ref[...], preferred_element_type=F32)
        gate = jnp.concatenate([t[cblk] for t in t_refs for cblk in range(t.shape[0])], axis=-1).astype(F32)
        return jax.nn.sigmoid(gate) * y

    merged = (branch(ga_ref, (ta0, ta1), pa_ref) + branch(gb_ref, (tb0, tb1), pb_ref)
              + branch(gc_ref, (tc0, tc1), pc_ref))
    y = jnp.dot(merged.astype(BF16), wo_ref[...], preferred_element_type=F32)
    out = h_ref[...] + gate_ref[0] * y
    if final:
        (o_ref,) = refs[16:]
        o_ref[...] = out * lax.rsqrt(jnp.mean(out * out, axis=-1, keepdims=True) + EPS) * nw_ref[...]
    else:
        shift_ref, scale_ref, o_ref, u_ref = refs[16:]
        o_ref[...] = out
        u_ref[...] = _modulated_norm(out, nw_ref[...], scale_ref[0], shift_ref[0])


def _merge(ga, gb, gc, proj, pa, pb, pc, wo, h, mod, layer, tail_norm_w, final):
    s, d = h.shape
    w = ga.shape[1]
    tm = 256
    const = dict(pipeline_mode=pl.Buffered(1))
    row_spec = pl.BlockSpec((tm, d), lambda i: (i, 0))
    tail_specs, tail_args = [], []
    out_specs, out_shape = row_spec, jax.ShapeDtypeStruct((s, d), F32)
    if not final:
        tail_specs = [pl.BlockSpec((1, 1, d), lambda i: (layer + 1, 0, 0)),
                      pl.BlockSpec((1, 1, d), lambda i: (layer + 1, 0, 1))]
        tail_args = [mod, mod]
        out_specs, out_shape = [row_spec, row_spec], [out_shape, jax.ShapeDtypeStruct((s, d), BF16)]
    return pl.pallas_call(
        functools.partial(_merge_body, final=final),
        grid=(s // tm,),
        out_specs=out_specs,
        out_shape=out_shape,
        in_specs=[
            pl.BlockSpec((tm, w), lambda i: (i, 0)),
            pl.BlockSpec((tm, w), lambda i: (i, 0)),
            pl.BlockSpec((tm, w), lambda i: (i, 0)),
            *[pl.BlockSpec((GATE_CB, tm, LANES), functools.partial(lambda i, b: (b, i, 0), b=CB_GA // GATE_CB + b))
              for b in range(3 * d // LANES // GATE_CB)],
            pl.BlockSpec((w, d), lambda i: (0, 0), **const),
            pl.BlockSpec((w, d), lambda i: (0, 0), **const),
            pl.BlockSpec((w, d), lambda i: (0, 0), **const),
            pl.BlockSpec((d, d), lambda i: (0, 0), **const),
            row_spec,
            pl.BlockSpec((1, 1, d), lambda i: (layer, 0, 2)),
            pl.BlockSpec((1, d), lambda i: (0, 0)),
            *tail_specs,
        ],
        compiler_params=_params("arbitrary"),
        name="merge",
    )(ga, gb, gc, *([proj] * 6), pa, pb, pc, wo, h, mod, tail_norm_w.reshape(1, d), *tail_args)


def kernel(x, c, w_in, w_proj_a, w_proj_b, w_proj_c, w_out, w_ada, b_ada, norm_w, lam, da_norm_w, ret_norm_w,
           final_norm_w):
    batch, s, d = x.shape
    assert batch == 1 and d == D_MODEL and s % DIL_TILE == 0 and w_in.shape[2] == IN_WIDTH
    depth = w_in.shape[0]
    mod = _adaln(c, w_ada, b_ada)
    h = x[0]
    u = _modnorm(h, norm_w[0], mod, 0)
    for layer in range(depth):
        final = layer == depth - 1
        lam_init = 0.8 - 0.6 * math.exp(-0.3 * layer)
        proj = _inproj(u, layer, w_in)
        ga = _diffattn(proj, lam[layer], da_norm_w[layer], lam_init)
        gb = _retention(proj, ret_norm_w[layer])
        gc = _dilated(proj)
        res = _merge(ga, gb, gc, proj,
                     w_proj_a[layer].astype(BF16), w_proj_b[layer].astype(BF16), w_proj_c[layer].astype(BF16),
                     w_out[layer].astype(BF16), h, mod, layer,
                     final_norm_w if final else norm_w[layer + 1], final)
        if final:
            return res[None]
        h, u = res
```
